```python
import math
import jax, jax.numpy as jnp
from jax import lax
import numpy as np

D_MODEL = 1024
BATCH = 2
SEQ = 8192
DEPTH = 2
DEC_BATCH = 32
DEC_SEQ = 4
PAST_LEN = 16384
PAGE_SIZE = 128

A_HEADS = 4
A_DIM = 64
A_WIDTH = A_HEADS * A_DIM
A_DECAY_LORA = 64
A_ICLR_LORA = 64
A_GATE_LORA = 128
A_COLS = 3 * A_WIDTH + A_DECAY_LORA + A_ICLR_LORA + A_GATE_LORA
A_LNX_EPS = 64e-5
B_HEADS = 4
B_DIM = 128
B_WIDTH = B_HEADS * B_DIM
B_QKV = 3 * B_WIDTH
B_COLS = B_QKV + B_WIDTH + 2 * B_HEADS
B_CONV = 4
B_CHUNK = 64
L2_EPS = 1e-6
C_HEADS = 4
C_DIM = 64
C_WIDTH = C_HEADS * C_DIM
C_COLS = 3 * C_WIDTH
C_BLOCK = 128
C_BIAS_INIT = -6.0
MIX_WIDTH = A_WIDTH + B_WIDTH + C_WIDTH
N_IN = A_COLS + B_COLS + C_COLS
D_FF = -(-8 * D_MODEL // (3 * 256)) * 256
N_ADA = 6
NORM_EPS = 1e-6

kernel_name = 'hybrid_rwkv7_gdn_stickbreak_adaln_step'


def _rmsnorm(x, g, eps=NORM_EPS):
    xf = x.astype(jnp.float32)
    return xf * lax.rsqrt(jnp.mean(xf * xf, axis=-1, keepdims=True) + eps) * g.astype(jnp.float32)


def _causal_conv(u, prev, w):
    T = u.shape[1]
    up = jnp.concatenate([prev.astype(jnp.float32), u], axis=1)
    y = up[:, 0:T] * w[0]
    for i in range(1, B_CONV):
        y = y + up[:, i:i + T] * w[i]
    return y, up[:, T:]


def _rwkv7(p, shift_prev, S0, mu, w0, w2, a0, a2, g2, k_k, k_a, r_k, lnx_w, lnx_b):
    Bn, T, _ = p.shape
    p_prev = jnp.concatenate([shift_prev[:, None].astype(jnp.float32), p[:, :-1]], axis=1)
    m = p + (p_prev - p) * mu
    r, k, v, pw, pa, pg = jnp.split(m, [A_WIDTH, 2 * A_WIDTH, 3 * A_WIDTH, 3 * A_WIDTH + A_DECAY_LORA,
                                        3 * A_WIDTH + A_DECAY_LORA + A_ICLR_LORA], axis=-1)
    w = -jax.nn.softplus(-(w0 + jnp.tanh(pw) @ w2)) - 0.5
    decay = jnp.exp(-jnp.exp(w))
    a = jax.nn.sigmoid(a0 + pa @ a2)
    g = jax.nn.sigmoid(pg) @ g2
    hd = lambda t: t.reshape(Bn, T, A_HEADS, A_DIM)
    kk = hd(k * k_k)
    kk = kk / jnp.maximum(jnp.linalg.norm(kk, axis=-1, keepdims=True), 1e-12)
    k = k * (1.0 + (a - 1.0) * k_a)
    r, k, v, a, decay = hd(r), hd(k), hd(v), hd(a), hd(decay)

    def step(S, inp):
        r_t, d_t, k_t, v_t, kk_t, a_t = inp
        sa = jnp.einsum('bhij,bhj->bhi', S, -kk_t)
        S = S * d_t[:, :, None, :] + sa[..., None] * (kk_t * a_t)[:, :, None, :] + v_t[..., None] * k_t[:, :, None, :]
        return S, jnp.einsum('bhij,bhj->bhi', S, r_t)

    S_T, y = lax.scan(step, S0.astype(jnp.float32),
                      tuple(jnp.moveaxis(t, 1, 0) for t in (r, decay, k, v, kk, a)))
    y = jnp.moveaxis(y, 0, 1)
    yc = y - jnp.mean(y, axis=-1, keepdims=True)
    y = yc * lax.rsqrt(jnp.mean(yc * yc, axis=-1, keepdims=True) + A_LNX_EPS)
    y = y.reshape(Bn, T, A_WIDTH) * lnx_w + lnx_b
    bonus = (jnp.sum(r * k * r_k, axis=-1, keepdims=True) * v).reshape(Bn, T, A_WIDTH)
    return (y + bonus) * g, p[:, -1], S_T


def _gated_delta(q, k, v, beta, g, S0, chunk):
    Bn, T, H, Dk = q.shape
    n = T // chunk

    def blocks(t):
        t = t.reshape((Bn, n, chunk, H) + t.shape[3:])
        return jnp.moveaxis(t, (1, 3), (0, 2))

    q = blocks(q) * Dk ** -0.5
    k, v, beta, g = blocks(k), blocks(v), blocks(beta), blocks(g)
    gc = jnp.cumsum(g, axis=-1)
    idx = jnp.arange(chunk)
    incl = idx[:, None] >= idx[None, :]
    strict = idx[:, None] > idx[None, :]
    decay = jnp.exp(jnp.where(incl, gc[..., :, None] - gc[..., None, :], -jnp.inf))
    kb = k * beta[..., None]
    A = jnp.where(strict, jnp.einsum('...id,...jd->...ij', kb, k) * decay, 0.0)
    L = A + jnp.eye(chunk, dtype=A.dtype)
    solve = lambda rhs: lax.linalg.triangular_solve(L, rhs, left_side=True, lower=True, unit_diagonal=True)
    u_v = solve(v * beta[..., None])
    u_s = solve(kb * jnp.exp(gc)[..., None])
    qk = jnp.einsum('...id,...jd->...ij', q, k) * decay
    qg = q * jnp.exp(gc)[..., None]
    kd = k * jnp.exp(gc[..., -1:] - gc)[..., None]
    glast = jnp.exp(gc[..., -1])

    def step(S, xs):
        u_v_c, u_s_c, qk_c, qg_c, kd_c, gl_c = xs
        w = u_v_c - u_s_c @ S
        o = qg_c @ S + qk_c @ w
        S = gl_c[..., None, None] * S + jnp.swapaxes(kd_c, -1, -2) @ w
        return S, o

    S_T, o = lax.scan(step, S0, (u_v, u_s, qk, qg, kd, glast))
    o = jnp.moveaxis(o, (0, 2), (1, 3)).reshape(Bn, T, H, v.shape[-1])
    return o, S_T


def _gdn(p, conv_prev, S0, conv_w, a_log, dt_bias, norm_w):
    Bn, T, _ = p.shape
    qkv, conv_new = _causal_conv(p[..., :B_QKV], conv_prev, conv_w.astype(jnp.float32))
    qkv = jax.nn.silu(qkv)
    q, k, v = (t.reshape(Bn, T, B_HEADS, B_DIM) for t in jnp.split(qkv, 3, axis=-1))
    l2 = lambda t: t * lax.rsqrt(jnp.sum(t * t, axis=-1, keepdims=True) + L2_EPS)
    z = p[..., B_QKV:B_QKV + B_WIDTH].reshape(Bn, T, B_HEADS, B_DIM)
    beta = jax.nn.sigmoid(p[..., B_QKV + B_WIDTH:B_QKV + B_WIDTH + B_HEADS])
    g = -jnp.exp(a_log.astype(jnp.float32)) * jax.nn.softplus(p[..., B_QKV + B_WIDTH + B_HEADS:] + dt_bias)
    chunk = B_CHUNK if T % B_CHUNK == 0 else T
    o, S_T = _gated_delta(l2(q), l2(k), v, beta, g, S0.astype(jnp.float32), chunk)
    o = _rmsnorm(o, norm_w) * jax.nn.silu(z)
    return o.reshape(Bn, T, B_WIDTH), conv_new, S_T


def _sb_partial(q, k, v, qpos, kpos, carry, bias):
    z = jnp.einsum('bqhd,bkhd->bhqk', q.astype(jnp.float32), k.astype(jnp.float32)) * (C_DIM ** -0.5)
    z = z + bias.astype(jnp.float32)[:, None, None]
    valid = kpos[None, :] < qpos[:, None]
    sp = jnp.where(valid, jax.nn.softplus(z), 0.0)
    rc = lax.cumsum(sp, axis=3, reverse=True)
    A = jnp.exp(jnp.where(valid, z - rc - carry[..., None], -jnp.inf))
    out = jnp.einsum('bhqk,bkhd->bqhd', A, v.astype(jnp.float32))
    return out, carry + rc[..., 0]


def _sb_prompt(q, k, v, bias):
    Bn, S, H, D = q.shape
    nb = S // C_BLOCK
    ar = jnp.arange(C_BLOCK)

    def q_block(i):
        q_i = lax.dynamic_slice_in_dim(q, i * C_BLOCK, C_BLOCK, axis=1)
        qpos = i * C_BLOCK + ar

        def body(j, acc):
            carry, out = acc
            kb = i - j
            k_j = lax.dynamic_slice_in_dim(k, kb * C_BLOCK, C_BLOCK, axis=1)
            v_j = lax.dynamic_slice_in_dim(v, kb * C_BLOCK, C_BLOCK, axis=1)
            o, carry = _sb_partial(q_i, k_j, v_j, qpos, kb * C_BLOCK + ar, carry, bias)
            return carry, out + o

        init = (jnp.zeros((Bn, H, C_BLOCK), jnp.float32), jnp.zeros((Bn, C_BLOCK, H, D), jnp.float32))
        return lax.fori_loop(0, i + 1, body, init)[1]

    blocks = lax.map(q_block, jnp.arange(nb))
    return jnp.moveaxis(blocks, 0, 1).reshape(Bn, S, H, D)


def _sb_sample(q, k, v, k_past, v_past, past_len, bias):
    Bn, T, H, _ = q.shape
    qpos = past_len + jnp.arange(T)
    o_new, carry = _sb_partial(q, k, v, qpos, qpos, jnp.zeros((Bn, H, T), jnp.float32), bias)
    o_past, _ = _sb_partial(q, k_past, v_past, qpos, jnp.arange(past_len), carry, bias)
    return o_new + o_past


def _layer(x, c, lp, shift0, wkv0, conv0, gdn0, past_kv):
    Bn, T, _ = x.shape
    mod = jax.nn.silu(c.astype(jnp.float32)) @ lp['w_ada'] + lp['b_ada']
    sh1, sc1, gt1, sh2, sc2, gt2 = jnp.split(mod[:, None, :], N_ADA, axis=-1)
    h = _rmsnorm(x, lp['norm1']) * (1.0 + sc1) + sh1
    proj = h @ lp['w_in']
    pA, pB, pC = jnp.split(proj, [A_COLS, A_COLS + B_COLS], axis=-1)
    yA, shift_new, wkv_new = _rwkv7(pA, shift0, wkv0, lp['a_mu'], lp['a_w0'], lp['a_w2'], lp['a_a0'],
                                    lp['a_a2'], lp['a_g2'], lp['a_k_k'], lp['a_k_a'], lp['a_r_k'],
                                    lp['a_lnx_w'], lp['a_lnx_b'])
    yB, conv_new, gdn_new = _gdn(pB, conv0, gdn0, lp['b_conv_w'], lp['b_a_log'], lp['b_dt_bias'], lp['b_norm_w'])
    q, k, v = (t.reshape(Bn, T, C_HEADS, C_DIM) for t in jnp.split(pC, 3, axis=-1))
    if past_kv is None:
        yC = _sb_prompt(q, k, v, lp['c_bias'])
    else:
        yC = _sb_sample(q, k, v, past_kv[0], past_kv[1], past_kv[2], lp['c_bias'])
    mix = jnp.concatenate([yA, yB, yC.reshape(Bn, T, C_WIDTH)], axis=-1)
    x = x + (gt1 * (mix @ lp['w_out'])).astype(x.dtype)
    h = _rmsnorm(x, lp['norm2']) * (1.0 + sc2) + sh2
    gate, up = jnp.split(h @ lp['w_gu'], 2, axis=-1)
    x = x + (gt2 * ((jax.nn.silu(gate) * up) @ lp['w_down'])).astype(x.dtype)
    return x, (k, v, shift_new, wkv_new, conv_new, gdn_new)


def setup_inputs(seed: int = 0) -> dict:
    key = jax.random.key(seed)
    ks = iter(jax.random.split(key, 48))
    nrm = lambda shape, s=1.0: jax.random.normal(next(ks), shape, jnp.float32) * s
    uni = lambda shape, lo, hi: jax.random.uniform(next(ks), shape, jnp.float32, lo, hi)
    n_pages = PAST_LEN // PAGE_SIZE
    n_used = DEC_BATCH * n_pages
    n_phys = n_used + (n_used + 3) // 4
    page_table = jax.random.permutation(next(ks), n_phys)[:n_used].reshape(DEC_BATCH, n_pages).astype(jnp.int32)
    dt = jnp.exp(uni((DEPTH, B_HEADS), math.log(1e-3), math.log(1e-1)))
    return {
        'x_prompt': nrm((BATCH, SEQ, D_MODEL)),
        'x_sample': nrm((DEC_BATCH, DEC_SEQ, D_MODEL)),
        'cache_k': nrm((DEPTH, n_phys, PAGE_SIZE, C_HEADS, C_DIM)),
        'cache_v': nrm((DEPTH, n_phys, PAGE_SIZE, C_HEADS, C_DIM)),
        'state_shift': nrm((DEPTH, DEC_BATCH, A_COLS)),
        'state_wkv': nrm((DEPTH, DEC_BATCH, A_HEADS, A_DIM, A_DIM), 0.5),
        'state_conv': nrm((DEPTH, DEC_BATCH, B_CONV - 1, B_QKV)),
        'state_gdn': nrm((DEPTH, DEC_BATCH, B_HEADS, B_DIM, B_DIM), 0.1),
        'page_table': page_table,
        'c_prompt': nrm((BATCH, D_MODEL)),
        'c_sample': nrm((DEC_BATCH, D_MODEL)),
        'w_ada': nrm((DEPTH, D_MODEL, N_ADA * D_MODEL), 0.5 * D_MODEL ** -0.5),
        'b_ada': nrm((DEPTH, N_ADA * D_MODEL), 0.02),
        'norm1': 1.0 + nrm((DEPTH, D_MODEL), 0.05),
        'norm2': 1.0 + nrm((DEPTH, D_MODEL), 0.05),
        'w_in': nrm((DEPTH, D_MODEL, N_IN), D_MODEL ** -0.5),
        'w_out': nrm((DEPTH, MIX_WIDTH, D_MODEL), MIX_WIDTH ** -0.5),
        'a_mu': uni((DEPTH, A_COLS), 0.0, 1.0),
        'a_w0': uni((DEPTH, A_WIDTH), -6.0, -1.0),
        'a_w2': nrm((DEPTH, A_DECAY_LORA, A_WIDTH), 0.1),
        'a_a0': nrm((DEPTH, A_WIDTH), 0.1),
        'a_a2': nrm((DEPTH, A_ICLR_LORA, A_WIDTH), A_ICLR_LORA ** -0.5),
        'a_g2': nrm((DEPTH, A_GATE_LORA, A_WIDTH), A_GATE_LORA ** -0.5),
        'a_k_k': 0.85 + nrm((DEPTH, A_WIDTH), 0.05),
        'a_k_a': 1.0 + nrm((DEPTH, A_WIDTH), 0.05),
        'a_r_k': nrm((DEPTH, A_HEADS, A_DIM), 0.1),
        'a_lnx_w': 1.0 + nrm((DEPTH, A_WIDTH), 0.05),
        'a_lnx_b': nrm((DEPTH, A_WIDTH), 0.02),
        'b_conv_w': nrm((DEPTH, B_CONV, B_QKV), B_CONV ** -0.5),
        'b_a_log': jnp.log(uni((DEPTH, B_HEADS), 1.0, 16.0)),
        'b_dt_bias': dt + jnp.log(-jnp.expm1(-dt)),
        'b_norm_w': 1.0 + nrm((DEPTH, B_DIM), 0.05),
        'c_bias': C_BIAS_INIT + nrm((DEPTH, C_HEADS), 0.1),
        'w_gu': nrm((DEPTH, D_MODEL, 2 * D_FF), D_MODEL ** -0.5),
        'w_down': nrm((DEPTH, D_FF, D_MODEL), D_FF ** -0.5),
        'norm_f': 1.0 + nrm((D_MODEL,), 0.05),
    }


def reference(x_prompt, x_sample, cache_k, cache_v, state_shift, state_wkv, state_conv, state_gdn,
              page_table, c_prompt, c_sample, w_ada, b_ada, norm1, norm2, w_in, w_out,
              a_mu, a_w0, a_w2, a_a0, a_a2, a_g2, a_k_k, a_k_a, a_r_k, a_lnx_w, a_lnx_b,
              b_conv_w, b_a_log, b_dt_bias, b_norm_w, c_bias, w_gu, w_down, norm_f):
    n_prompt = x_prompt.shape[0]
    n_dec = x_sample.shape[0]
    past_len = page_table.shape[1] * cache_k.shape[2]
    f32 = jnp.float32
    xp, xs = x_prompt, x_sample
    p_out = [[] for _ in range(6)]
    s_out = [[] for _ in range(6)]
    for l in range(DEPTH):
        lp = dict(w_ada=w_ada[l], b_ada=b_ada[l], norm1=norm1[l], norm2=norm2[l], w_in=w_in[l], w_out=w_out[l],
                  a_mu=a_mu[l], a_w0=a_w0[l], a_w2=a_w2[l], a_a0=a_a0[l], a_a2=a_a2[l], a_g2=a_g2[l],
                  a_k_k=a_k_k[l], a_k_a=a_k_a[l], a_r_k=a_r_k[l], a_lnx_w=a_lnx_w[l], a_lnx_b=a_lnx_b[l],
                  b_conv_w=b_conv_w[l], b_a_log=b_a_log[l], b_dt_bias=b_dt_bias[l], b_norm_w=b_norm_w[l],
                  c_bias=c_bias[l], w_gu=w_gu[l], w_down=w_down[l])
        xp, st_p = _layer(xp, c_prompt, lp,
                          jnp.zeros((n_prompt, A_COLS), f32),
                          jnp.zeros((n_prompt, A_HEADS, A_DIM, A_DIM), f32),
                          jnp.zeros((n_prompt, B_CONV - 1, B_QKV), f32),
                          jnp.zeros((n_prompt, B_HEADS, B_DIM, B_DIM), f32),
                          None)
        k_past = cache_k[l][page_table].reshape(n_dec, past_len, C_HEADS, C_DIM)
        v_past = cache_v[l][page_table].reshape(n_dec, past_len, C_HEADS, C_DIM)
        xs, st_s = _layer(xs, c_sample, lp, state_shift[l], state_wkv[l], state_conv[l], state_gdn[l],
                          (k_past, v_past, past_len))
        for i in range(6):
            p_out[i].append(st_p[i].astype(x_prompt.dtype))
            s_out[i].append(st_s[i].astype(x_sample.dtype))
    y_prompt = _rmsnorm(xp, norm_f).astype(x_prompt.dtype)
    y_sample = _rmsnorm(xs, norm_f).astype(x_sample.dtype)
    p_k, p_v, p_shift, p_wkv, p_conv, p_gdn = (jnp.stack(t) for t in p_out)
    s_k, s_v, s_shift, s_wkv, s_conv, s_gdn = (jnp.stack(t) for t in s_out)
    return (y_prompt, y_sample, p_k, p_v, p_shift, p_wkv, p_conv, p_gdn, s_k, s_v, s_shift, s_wkv, s_conv, s_gdn)
```

```python
import functools

import jax
import jax.numpy as jnp
from jax import lax
from jax.experimental import pallas as pl
from jax.experimental.pallas import tpu as pltpu

F32 = jnp.float32
BF16 = jnp.bfloat16
HI = lax.Precision.HIGHEST

NORM_EPS = 1e-6
L2_EPS = 1e-6
A_LNX_EPS = 64e-5

A_HEADS, A_DIM = 4, 64
A_WIDTH = A_HEADS * A_DIM
B_HEADS, B_DIM = 4, 128
B_WIDTH = B_HEADS * B_DIM
B_QKV = 3 * B_WIDTH
B_CONV = 4
C_HEADS, C_DIM = 4, 64
C_WIDTH = C_HEADS * C_DIM

NP = 4096
COL_A = 0
COL_BZ = 1024
COL_BQKV = 1536
COL_C = 3072
COL_BG = 3840

LANE = 128
SUB = 8
VMEM_LIMIT = 56 * 1024 * 1024


def _dot(a, b, prec=None):
    return jnp.dot(a, b, preferred_element_type=F32, precision=prec)


def _dot_nt(a, b, prec=None):
    return lax.dot_general(a, b, (((1,), (1,)), ((), ())), preferred_element_type=F32, precision=prec)


def _dot_tn(a, b, prec=None):
    return lax.dot_general(a, b, (((0,), (0,)), ((), ())), preferred_element_type=F32, precision=prec)


def _iota2(shape, dim):
    return lax.broadcasted_iota(jnp.int32, shape, dim)


def _sigmoid(x):
    return jax.nn.sigmoid(x)


def _softplus(x):
    return jnp.maximum(x, 0.0) + jnp.log1p(jnp.exp(-jnp.abs(x)))


def _unit_lower_inverse(a, n):
    eye = (_iota2((n, n), 0) == _iota2((n, n), 1)).astype(F32)
    m = -a
    t = eye + m
    p = m
    k = 2
    while k < n:
        p = _dot(p, p, HI)
        t = t + _dot(t, p, HI)
        k *= 2
    return t


def _params(sem):
    return pltpu.CompilerParams(dimension_semantics=sem, vmem_limit_bytes=VMEM_LIMIT)


def _ada_kernel(c_ref, w_ref, b_ref, o_ref):
    c = c_ref[...]
    s = c * _sigmoid(c)
    o_ref[0] = _dot(s.astype(BF16), w_ref[0].astype(BF16)) + b_ref[0]


def _ada(c_all, w_ada, b_ada):
    depth, d, n = w_ada.shape
    r = c_all.shape[0]
    tn = 1024
    return pl.pallas_call(
        _ada_kernel,
        grid=(depth, n // tn),
        in_specs=[pl.BlockSpec((r, d), lambda l, j: (0, 0)),
                  pl.BlockSpec((1, d, tn), lambda l, j: (l, 0, j)),
                  pl.BlockSpec((1, 1, tn), lambda l, j: (l, 0, j))],
        out_specs=pl.BlockSpec((1, r, tn), lambda l, j: (l, 0, j)),
        out_shape=jax.ShapeDtypeStruct((depth, r, n), F32),
        compiler_params=_params(("parallel", "parallel")),
        name="ada",
    )(c_all, w_ada, b_ada.reshape(depth, 1, n))


def _rms_mod(x, g, sc, sh):
    ms = jnp.mean(x * x, axis=-1, keepdims=True)
    h = x * lax.rsqrt(ms + NORM_EPS) * g
    return h * (1.0 + sc) + sh


def _inproj_kernel(x_ref, g_ref, sc_ref, sh_ref, w_ref, o_ref, *, nchunk):
    h = _rms_mod(x_ref[0], g_ref[...], sc_ref[0], sh_ref[0]).astype(BF16)
    for n in range(NP // nchunk):
        o_ref[0, :, n * nchunk:(n + 1) * nchunk] = _dot(h, w_ref[:, n * nchunk:(n + 1) * nchunk])


def _mod_spec(mod, tm, d, ngrid):
    per_row = mod.shape[1] != 1
    if ngrid == 2:
        imap = (lambda b, i: (b, i, 0)) if per_row else (lambda b, i: (b, 0, 0))
    else:
        imap = (lambda b, i, f: (b, i, 0)) if per_row else (lambda b, i, f: (b, 0, 0))
    return pl.BlockSpec((1, tm if per_row else 1, d), imap)


def _inproj(x, g, sc, sh, w, tm):
    bn, t, d = x.shape
    return pl.pallas_call(
        functools.partial(_inproj_kernel, nchunk=1024),
        grid=(bn, t // tm),
        in_specs=[pl.BlockSpec((1, tm, d), lambda b, i: (b, i, 0)),
                  pl.BlockSpec((1, d), lambda b, i: (0, 0)),
                  _mod_spec(sc, tm, d, 2),
                  _mod_spec(sh, tm, d, 2),
                  pl.BlockSpec((d, NP), lambda b, i: (0, 0))],
        out_specs=pl.BlockSpec((1, tm, NP), lambda b, i: (b, i, 0)),
        out_shape=jax.ShapeDtypeStruct((bn, t, NP), F32),
        compiler_params=_params(("parallel", "parallel")),
        name="inproj",
    )(x, g.reshape(1, d), sc, sh, w)


def _tail_kernel(x_ref, ya_ref, yb_ref, yc_ref, gt1_ref, sh2_ref, sc2_ref, gt2_ref, g2_ref, gf_ref,
                 wo_ref, wg_ref, wu_ref, wd_ref, o_ref, x1_s, h_s, acc_s, *, nf, final_norm):
    f = pl.program_id(2)

    @pl.when(f == 0)
    def _():
        mix = (_dot(ya_ref[0].astype(BF16), wo_ref[0:A_WIDTH])
               + _dot(yb_ref[0].astype(BF16), wo_ref[A_WIDTH:A_WIDTH + B_WIDTH])
               + _dot(yc_ref[0].astype(BF16), wo_ref[A_WIDTH + B_WIDTH:A_WIDTH + B_WIDTH + C_WIDTH]))
        x1 = x_ref[0] + gt1_ref[0] * mix
        x1_s[...] = x1
        h_s[...] = _rms_mod(x1, g2_ref[...], sc2_ref[0], sh2_ref[0]).astype(BF16)
        acc_s[...] = jnp.zeros_like(acc_s)

    hb = h_s[...]
    gate = _dot(hb, wg_ref[...])
    up = _dot(hb, wu_ref[...])
    act = (gate * _sigmoid(gate)) * up
    acc_s[...] += _dot(act.astype(BF16), wd_ref[...])

    @pl.when(f == nf - 1)
    def _():
        y = x1_s[...] + gt2_ref[0] * acc_s[...]
        if final_norm:
            ms = jnp.mean(y * y, axis=-1, keepdims=True)
            y = y * lax.rsqrt(ms + NORM_EPS) * gf_ref[...]
        o_ref[0] = y


def _tail(x, ya, yb, yc, gt1, sh2, sc2, gt2, g2, gf, wo, wgu, wd, tm, tf, final_norm):
    bn, t, d = x.shape
    dff = wd.shape[0]
    nf = dff // tf
    row = lambda w: pl.BlockSpec((1, tm, w), lambda b, i, f: (b, i, 0))
    vec = pl.BlockSpec((1, d), lambda b, i, f: (0, 0))
    return pl.pallas_call(
        functools.partial(_tail_kernel, nf=nf, final_norm=final_norm),
        grid=(bn, t // tm, nf),
        in_specs=[row(d), row(A_WIDTH), row(B_WIDTH), row(C_WIDTH),
                  _mod_spec(gt1, tm, d, 3), _mod_spec(sh2, tm, d, 3), _mod_spec(sc2, tm, d, 3),
                  _mod_spec(gt2, tm, d, 3), vec, vec,
                  pl.BlockSpec((d, d), lambda b, i, f: (0, 0)),
                  pl.BlockSpec((d, tf), lambda b, i, f: (0, f)),
                  pl.BlockSpec((d, tf), lambda b, i, f: (0, f + nf)),
                  pl.BlockSpec((tf, d), lambda b, i, f: (f, 0))],
        out_specs=pl.BlockSpec((1, tm, d), lambda b, i, f: (b, i, 0)),
        out_shape=jax.ShapeDtypeStruct((bn, t, d), F32),
        scratch_shapes=[pltpu.VMEM((tm, d), F32), pltpu.VMEM((tm, d), BF16), pltpu.VMEM((tm, d), F32)],
        compiler_params=_params(("parallel", "parallel", "arbitrary")),
        name="tail",
    )(x, ya, yb, yc, gt1, sh2, sc2, gt2, g2.reshape(1, d), gf.reshape(1, d), wo, wgu, wgu, wd)


def _rwkv_kernel(p_ref, sh0_ref, s0_ref, mu_ref, w0_ref, w2_ref, a0_ref, a2_ref, g2_ref, kk_ref, ka_ref,
                 rk_ref, lw_ref, lb_ref, y_ref, sn_ref, ext_s, st_s, y_s, *, tb, c):
    ci = pl.program_id(1)
    nw = A_WIDTH

    @pl.when(ci == 0)
    def _():
        ext_s[SUB - 1:SUB] = sh0_ref[0]
        st_s[...] = s0_ref[0]

    ext_s[SUB:SUB + tb] = p_ref[0]
    if tb < c:
        ext_s[SUB + tb:SUB + c] = jnp.zeros((c - tb, ext_s.shape[1]), F32)
    p = ext_s[SUB:SUB + c]
    p_prev = ext_s[SUB - 1:SUB - 1 + c]
    m = p + (p_prev - p) * mu_ref[...]
    ext_s[SUB - 1:SUB] = p[tb - 1:tb]

    r = m[:, 0:nw]
    k = m[:, nw:2 * nw]
    v = m[:, 2 * nw:3 * nw]
    pw = m[:, 3 * nw:3 * nw + 64]
    pa = m[:, 3 * nw + 64:3 * nw + 128]
    pg = m[:, 3 * nw + 128:3 * nw + 256]

    w = -_softplus(-(w0_ref[...] + _dot(jnp.tanh(pw), w2_ref[...], HI))) - 0.5
    logd = -jnp.exp(w)
    a = _sigmoid(a0_ref[...] + _dot(pa, a2_ref[...], HI))
    g = _dot(_sigmoid(pg), g2_ref[...], HI)

    head_of = lambda n, dim: _iota2((nw, nw), dim) // A_DIM
    ones_blk = (head_of(nw, 0) == head_of(nw, 1)).astype(F32)
    hsum = lambda t: _dot(t, ones_blk, HI)

    kkf = k * kk_ref[...]
    kk = kkf / jnp.maximum(jnp.sqrt(hsum(kkf * kkf)), 1e-12)
    k2 = k * (1.0 + (a - 1.0) * ka_ref[...])

    if tb < c:
        valid = _iota2((c, nw), 0) < tb
        zero = jnp.zeros((c, nw), F32)
        logd = jnp.where(valid, logd, zero)
        kk = jnp.where(valid, kk, zero)
        k2 = jnp.where(valid, k2, zero)
        v = jnp.where(valid, v, zero)

    rows = _iota2((c, c), 0)
    cols = _iota2((c, c), 1)
    incl = rows >= cols
    strict = rows > cols
    cum = _dot(incl.astype(F32), logd, HI)
    e_in = jnp.exp(cum)
    e_ex = jnp.exp(cum - logd)
    e_neg = jnp.exp(-cum)
    kkt = kk * e_ex
    rt = r * e_in
    kah = kk * a * e_neg
    kh = k2 * e_neg
    pc = e_in[c - 1:c]

    zc = jnp.zeros((c, c), F32)
    for h in range(A_HEADS):
        sl = slice(h * A_DIM, (h + 1) * A_DIM)
        kkt_h, rt_h, kah_h, kh_h, v_h, pc_h = kkt[:, sl], rt[:, sl], kah[:, sl], kh[:, sl], v[:, sl], pc[:, sl]
        s = st_s[h]
        la = jnp.where(strict, _dot_nt(kkt_h, kah_h, HI), zc)
        lk = jnp.where(strict, _dot_nt(kkt_h, kh_h, HI), zc)
        ma = jnp.where(incl, _dot_nt(rt_h, kah_h, HI), zc)
        mk = jnp.where(incl, _dot_nt(rt_h, kh_h, HI), zc)
        rhs = -(_dot_nt(kkt_h, s, HI) + _dot(lk, v_h, HI))
        sa = _dot(_unit_lower_inverse(la, c), rhs, HI)
        y_h = _dot_nt(rt_h, s, HI) + _dot(ma, sa, HI) + _dot(mk, v_h, HI)
        st_s[h] = s * pc_h + _dot_tn(sa, kah_h * pc_h, HI) + _dot_tn(v_h, kh_h * pc_h, HI)
        y_s[:, sl] = y_h

    y = y_s[...]
    mean = hsum(y) * (1.0 / A_DIM)
    yc = y - mean
    var = hsum(yc * yc) * (1.0 / A_DIM)
    yn = yc * lax.rsqrt(var + A_LNX_EPS) * lw_ref[...] + lb_ref[...]
    bonus = hsum(r * k2 * rk_ref[...]) * v
    out = (yn + bonus) * g
    y_ref[0] = out[0:tb]
    sn_ref[0] = st_s[...]


def _rwkv(proj, shift0, s0, lp, c):
    bn, t, _ = proj.shape
    tb = min(t, c)
    cols = 4 * A_WIDTH
    vec = lambda n: pl.BlockSpec((1, n), lambda b, i: (0, 0))
    mat = lambda r, n: pl.BlockSpec((r, n), lambda b, i: (0, 0))
    row = lambda a: a.reshape(1, -1)
    return pl.pallas_call(
        functools.partial(_rwkv_kernel, tb=tb, c=c),
        grid=(bn, t // tb),
        in_specs=[pl.BlockSpec((1, tb, cols), lambda b, i: (b, i, COL_A // cols)),
                  pl.BlockSpec((1, 1, cols), lambda b, i: (b, 0, 0)),
                  pl.BlockSpec((1, A_HEADS, A_DIM, A_DIM), lambda b, i: (b, 0, 0, 0)),
                  vec(cols), vec(A_WIDTH), mat(64, A_WIDTH), vec(A_WIDTH), mat(64, A_WIDTH), mat(128, A_WIDTH),
                  vec(A_WIDTH), vec(A_WIDTH), vec(A_WIDTH), vec(A_WIDTH), vec(A_WIDTH)],
        out_specs=[pl.BlockSpec((1, tb, A_WIDTH), lambda b, i: (b, i, 0)),
                   pl.BlockSpec((1, A_HEADS, A_DIM, A_DIM), lambda b, i: (b, 0, 0, 0))],
        out_shape=[jax.ShapeDtypeStruct((bn, t, A_WIDTH), F32),
                   jax.ShapeDtypeStruct((bn, A_HEADS, A_DIM, A_DIM), F32)],
        scratch_shapes=[pltpu.VMEM((SUB + c, cols), F32),
                        pltpu.VMEM((A_HEADS, A_DIM, A_DIM), F32),
                        pltpu.VMEM((c, A_WIDTH), F32)],
        compiler_params=_params(("parallel", "arbitrary")),
        name="rwkv",
    )(proj, shift0.reshape(bn, 1, cols), s0, row(lp["a_mu"]), row(lp["a_w0"]), lp["a_w2"], row(lp["a_a0"]),
      lp["a_a2"], lp["a_g2"], row(lp["a_k_k"]), row(lp["a_k_a"]), row(lp["a_r_k"]), row(lp["a_lnx_w"]),
      row(lp["a_lnx_b"]))


def _gdn_kernel(qkv_ref, z_ref, bg_ref, cv0_ref, s0_ref, cw_ref, alog_ref, dtb_ref, nw_ref,
                y_ref, sn_ref, ext_s, st_s, bg_s, *, tb, c):
    ci = pl.program_id(1)

    @pl.when(ci == 0)
    def _():
        ext_s[0:SUB] = cv0_ref[0]
        st_s[...] = s0_ref[0]

    ext_s[SUB:SUB + tb] = qkv_ref[0]
    if tb < c:
        ext_s[SUB + tb:SUB + c] = jnp.zeros((c - tb, ext_s.shape[1]), F32)
    cw = cw_ref[...]
    u = ext_s[SUB - 3:SUB - 3 + c] * cw[0:1]
    for i in range(1, B_CONV):
        u = u + ext_s[SUB - 3 + i:SUB - 3 + i + c] * cw[i:i + 1]
    carry = ext_s[tb:tb + SUB]
    ext_s[0:SUB] = carry

    qkv = u * _sigmoid(u)
    nb = bg_s.shape[1]
    bg_s[0:tb] = bg_ref[0]
    if tb < c:
        bg_s[tb:c] = jnp.zeros((c - tb, nb), F32)
    bg = bg_s[...]
    beta_all = _sigmoid(bg)
    gl = -jnp.exp(alog_ref[...]) * _softplus(bg + dtb_ref[...])
    if tb < c:
        valid = _iota2((c, nb), 0) < tb
        beta_all = jnp.where(valid, beta_all, jnp.zeros_like(beta_all))
        gl = jnp.where(valid, gl, jnp.zeros_like(gl))

    rows = _iota2((c, c), 0)
    cols = _iota2((c, c), 1)
    incl = rows >= cols
    strict = rows > cols
    zc = jnp.zeros((c, c), F32)
    gl = gl[:, 0:LANE]
    gc = _dot(incl.astype(F32), gl, HI)
    gct = _dot_tn(gl, (rows <= cols).astype(F32), HI)

    z = z_ref[0]
    for h in range(B_HEADS):
        sl = slice(h * B_DIM, (h + 1) * B_DIM)
        q_h = qkv[:, h * B_DIM:(h + 1) * B_DIM]
        k_h = qkv[:, B_WIDTH + h * B_DIM:B_WIDTH + (h + 1) * B_DIM]
        v_h = qkv[:, 2 * B_WIDTH + h * B_DIM:2 * B_WIDTH + (h + 1) * B_DIM]
        q_h = q_h * lax.rsqrt(jnp.sum(q_h * q_h, axis=-1, keepdims=True) + L2_EPS) * (B_DIM ** -0.5)
        k_h = k_h * lax.rsqrt(jnp.sum(k_h * k_h, axis=-1, keepdims=True) + L2_EPS)
        gcol = gc[:, B_HEADS + h:B_HEADS + h + 1]
        grow = gct[B_HEADS + h:B_HEADS + h + 1, :]
        bcol = beta_all[:, h:h + 1]
        dm = gcol - grow
        decay = jnp.where(incl, jnp.exp(jnp.where(incl, dm, zc)), zc)
        kb = k_h * bcol
        amat = jnp.where(strict, _dot_nt(kb, k_h, HI) * decay, zc)
        tinv = _unit_lower_inverse(amat, c)
        eg = jnp.exp(gcol)
        u_v = _dot(tinv, v_h * bcol, HI)
        u_s = _dot(tinv, kb * eg, HI)
        qk = _dot_nt(q_h, k_h, HI) * decay
        s = st_s[h]
        w = u_v - _dot(u_s, s, HI)
        o = _dot(q_h * eg, s, HI) + _dot(qk, w, HI)
        glast = gc[c - 1:c, B_HEADS + h:B_HEADS + h + 1]
        kd = k_h * jnp.exp(glast - gcol)
        st_s[h] = jnp.exp(glast) * s + _dot_tn(kd, w, HI)
        on = o * lax.rsqrt(jnp.mean(o * o, axis=-1, keepdims=True) + NORM_EPS) * nw_ref[...]
        o_t = on[0:tb]
        z_h = z[:, sl]
        y_ref[0, :, sl] = o_t * (z_h * _sigmoid(z_h))
    sn_ref[0] = st_s[...]


def _gdn(proj, conv0, s0, lp, c):
    bn, t, _ = proj.shape
    tb = min(t, c)
    nb = NP - COL_BG
    cv0 = jnp.concatenate([jnp.zeros((bn, SUB - (B_CONV - 1), B_QKV), F32), conv0], axis=1)
    lane8 = lambda a: jnp.zeros((1, nb), F32).at[0, B_HEADS:2 * B_HEADS].set(a)
    return pl.pallas_call(
        functools.partial(_gdn_kernel, tb=tb, c=c),
        grid=(bn, t // tb),
        in_specs=[pl.BlockSpec((1, tb, B_QKV), lambda b, i: (b, i, COL_BQKV // B_QKV)),
                  pl.BlockSpec((1, tb, B_WIDTH), lambda b, i: (b, i, COL_BZ // B_WIDTH)),
                  pl.BlockSpec((1, tb, nb), lambda b, i: (b, i, COL_BG // nb)),
                  pl.BlockSpec((1, SUB, B_QKV), lambda b, i: (b, 0, 0)),
                  pl.BlockSpec((1, B_HEADS, B_DIM, B_DIM), lambda b, i: (b, 0, 0, 0)),
                  pl.BlockSpec((B_CONV, B_QKV), lambda b, i: (0, 0)),
                  pl.BlockSpec((1, nb), lambda b, i: (0, 0)),
                  pl.BlockSpec((1, nb), lambda b, i: (0, 0)),
                  pl.BlockSpec((1, B_DIM), lambda b, i: (0, 0))],
        out_specs=[pl.BlockSpec((1, tb, B_WIDTH), lambda b, i: (b, i, 0)),
                   pl.BlockSpec((1, B_HEADS, B_DIM, B_DIM), lambda b, i: (b, 0, 0, 0))],
        out_shape=[jax.ShapeDtypeStruct((bn, t, B_WIDTH), F32),
                   jax.ShapeDtypeStruct((bn, B_HEADS, B_DIM, B_DIM), F32)],
        scratch_shapes=[pltpu.VMEM((SUB + c, B_QKV), F32),
                        pltpu.VMEM((B_HEADS, B_DIM, B_DIM), F32),
                        pltpu.VMEM((c, nb), F32)],
        compiler_params=_params(("parallel", "arbitrary")),
        name="gdn",
    )(proj, proj, proj, cv0, s0, lp["b_conv_w"], lane8(lp["b_a_log"]), lane8(lp["b_dt_bias"]),
      lp["b_norm_w"].reshape(1, B_DIM))


def _split3(x):
    hi = x.astype(BF16)
    r1 = x - hi.astype(F32)
    mid = r1.astype(BF16)
    lo = (r1 - mid.astype(F32)).astype(BF16)
    return hi, mid, lo


def _sb_tile(zt, valid, carry, tri, v_h):
    sp = _softplus(zt)
    if valid is not None:
        sp = jnp.where(valid, sp, jnp.zeros_like(sp))
    hi, mid, lo = _split3(sp)
    rc = _dot(tri, hi) + _dot(tri, mid) + _dot(tri, lo)
    at = jnp.exp(zt - rc - carry)
    if valid is not None:
        at = jnp.where(valid, at, jnp.zeros_like(at))
    return _dot_tn(v_h, at.astype(BF16)), carry + rc[0:1]


def _sbp_kernel(bias_ref, q_ref, k_ref, v_ref, o_ref, carry_s, acc_s, *, tq, tk):
    qi = pl.program_id(1)
    j = pl.program_id(2)
    nsub = tq // tk

    @pl.when(j == 0)
    def _():
        carry_s[...] = jnp.zeros_like(carry_s)
        acc_s[...] = jnp.zeros_like(acc_s)

    def sweep(diag):
        qb = (q_ref[0] * (C_DIM ** -0.5)).astype(BF16)
        kb = k_ref[0].astype(BF16)
        vb = v_ref[0].astype(BF16)
        tri = (_iota2((tk, tk), 1) >= _iota2((tk, tk), 0)).astype(BF16)
        for h in range(C_HEADS):
            sl = slice(h * C_DIM, (h + 1) * C_DIM)
            q_h = qb[:, sl]
            carry = carry_s[h:h + 1, :]
            acc = acc_s[h]
            for s in reversed(range(nsub)):
                k_h = kb[s * tk:(s + 1) * tk, sl]
                v_h = vb[s * tk:(s + 1) * tk, sl]
                zt = _dot_nt(k_h, q_h) + bias_ref[h]
                valid = (s * tk + _iota2((tk, tq), 0)) < _iota2((tk, tq), 1) if diag else None
                o_t, carry = _sb_tile(zt, valid, carry, tri, v_h)
                acc = acc + o_t
            carry_s[h:h + 1, :] = carry
            acc_s[h] = acc

    @pl.when(j == 0)
    def _():
        sweep(True)

    @pl.when(jnp.logical_and(j > 0, j <= qi))
    def _():
        sweep(False)

    @pl.when(j == qi)
    def _():
        o_ref[0] = acc_s[...].reshape(C_WIDTH, tq).T


def _sb_prompt(proj, bias, tq, tk):
    bn, t, _ = proj.shape
    nq = t // tq
    qcol = COL_C // C_WIDTH
    return pl.pallas_call(
        functools.partial(_sbp_kernel, tq=tq, tk=tk),
        grid_spec=pltpu.PrefetchScalarGridSpec(
            num_scalar_prefetch=0,
            grid=(bn, nq, nq),
            in_specs=[pl.BlockSpec(memory_space=pltpu.SMEM),
                      pl.BlockSpec((1, tq, C_WIDTH), lambda b, i, j: (b, i, qcol)),
                      pl.BlockSpec((1, tq, C_WIDTH), lambda b, i, j: (b, jnp.maximum(i - j, 0), qcol + 1)),
                      pl.BlockSpec((1, tq, C_WIDTH), lambda b, i, j: (b, jnp.maximum(i - j, 0), qcol + 2))],
            out_specs=pl.BlockSpec((1, tq, C_WIDTH), lambda b, i, j: (b, i, 0)),
            scratch_shapes=[pltpu.VMEM((C_HEADS, tq), F32), pltpu.VMEM((C_HEADS, C_DIM, tq), F32)]),
        out_shape=jax.ShapeDtypeStruct((bn, t, C_WIDTH), F32),
        compiler_params=_params(("parallel", "parallel", "arbitrary")),
        name="sbp",
    )(bias, proj, proj, proj)


def _sbs_kernel(pt_ref, bias_ref, q_ref, k_ref, v_ref, *rest, tn, npage, ps):
    kp_refs = rest[0:npage]
    vp_refs = rest[npage:2 * npage]
    o_ref, q_s, carry_s, acc_s, kown_s, vown_s = rest[2 * npage:]
    j = pl.program_id(1)
    nj = pl.num_programs(1)
    rows = C_HEADS * SUB
    tri = (_iota2((ps, ps), 0) >= _iota2((ps, ps), 1)).astype(BF16)
    bias_col = jnp.concatenate([jnp.full((SUB, 1), bias_ref[h], F32) for h in range(C_HEADS)], axis=0)

    def block(z, valid, a_times_v):
        sp = _softplus(z)
        if valid is not None:
            sp = jnp.where(valid, sp, jnp.zeros_like(sp))
        hi, mid, lo = _split3(sp)
        rc = _dot(hi, tri) + _dot(mid, tri) + _dot(lo, tri)
        a = jnp.exp(z - rc - carry_s[...])
        if valid is not None:
            a = jnp.where(valid, a, jnp.zeros_like(a))
        ab = a.astype(BF16)
        for h in range(C_HEADS):
            acc_s[h] += a_times_v(h, ab[h * SUB:(h + 1) * SUB])
        carry_s[...] += rc[:, 0:1]

    @pl.when(j == 0)
    def _():
        q_s[...] = jnp.zeros_like(q_s)
        q_s[0:tn] = q_ref[0] * (C_DIM ** -0.5)
        kown_s[...] = jnp.zeros_like(kown_s)
        vown_s[...] = jnp.zeros_like(vown_s)
        kown_s[0:tn] = k_ref[0]
        vown_s[0:tn] = v_ref[0]
        carry_s[...] = jnp.zeros_like(carry_s)
        acc_s[...] = jnp.zeros_like(acc_s)
        qb = q_s[...].astype(BF16)
        kb = kown_s[...].astype(BF16)
        vb = vown_s[...].astype(BF16)
        z = jnp.concatenate([_dot_nt(qb[:, h * C_DIM:(h + 1) * C_DIM], kb[:, h * C_DIM:(h + 1) * C_DIM])
                             for h in range(C_HEADS)], axis=0) + bias_col
        qidx = _iota2((rows, ps), 0) % SUB
        valid = _iota2((rows, ps), 1) < qidx
        block(z, valid, lambda h, a_h: _dot(a_h, vb[:, h * C_DIM:(h + 1) * C_DIM]))

    qb = q_s[...].astype(BF16)
    for i in range(npage):
        kt = kp_refs[i][0, 0].astype(BF16)
        vt = vp_refs[i][0, 0].astype(BF16)
        z = jnp.concatenate([_dot(qb[:, h * C_DIM:(h + 1) * C_DIM], kt[h]) for h in range(C_HEADS)],
                            axis=0) + bias_col
        block(z, None, lambda h, a_h: _dot_nt(a_h, vt[h]))

    @pl.when(j == nj - 1)
    def _():
        for h in range(C_HEADS):
            o_ref[0, :, h * C_DIM:(h + 1) * C_DIM] = acc_s[h][0:tn]


def _sb_sample(proj, cache_kt, cache_vt, page_table, bias, layer, npage):
    bn, tn, _ = proj.shape
    n_pages = page_table.shape[1]
    ps = cache_kt.shape[-1]
    qcol = COL_C // C_WIDTH
    page_spec = lambda i: pl.BlockSpec(
        (1, 1, C_HEADS, C_DIM, ps),
        lambda b, j, pt: (layer, pt[b, n_pages - 1 - (j * npage + i)], 0, 0, 0))
    own = lambda off: pl.BlockSpec((1, tn, C_WIDTH), lambda b, j, pt: (b, 0, qcol + off))
    return pl.pallas_call(
        functools.partial(_sbs_kernel, tn=tn, npage=npage, ps=ps),
        grid_spec=pltpu.PrefetchScalarGridSpec(
            num_scalar_prefetch=1,
            grid=(bn, n_pages // npage),
            in_specs=[pl.BlockSpec(memory_space=pltpu.SMEM), own(0), own(1), own(2)]
                     + [page_spec(i) for i in range(npage)] + [page_spec(i) for i in range(npage)],
            out_specs=pl.BlockSpec((1, tn, C_WIDTH), lambda b, j, pt: (b, 0, 0)),
            scratch_shapes=[pltpu.VMEM((SUB, C_WIDTH), F32),
                            pltpu.VMEM((C_HEADS * SUB, 1), F32),
                            pltpu.VMEM((C_HEADS, SUB, C_DIM), F32),
                            pltpu.VMEM((ps, C_WIDTH), F32),
                            pltpu.VMEM((ps, C_WIDTH), F32)]),
        out_shape=jax.ShapeDtypeStruct((bn, tn, C_WIDTH), F32),
        compiler_params=_params(("parallel", "arbitrary")),
        name="sbs",
    )(page_table, bias, proj, proj, proj, *([cache_kt] * npage), *([cache_vt] * npage))


def _reorder_w_in(w):
    a_cols = 4 * A_WIDTH
    b0 = a_cols
    bz0 = b0 + B_QKV
    bg0 = bz0 + B_WIDTH
    c0 = bg0 + 2 * B_HEADS
    c1 = c0 + 3 * C_WIDTH
    pad = jnp.zeros((w.shape[0], NP - (COL_BG + 2 * B_HEADS)), w.dtype)
    return jnp.concatenate([w[:, 0:a_cols], w[:, bz0:bg0], w[:, b0:bz0], w[:, c0:c1], w[:, bg0:c0], pad],
                           axis=1).astype(BF16)


def kernel(x_prompt, x_sample, cache_k, cache_v, state_shift, state_wkv, state_conv, state_gdn, page_table,
           c_prompt, c_sample, w_ada, b_ada, norm1, norm2, w_in, w_out, a_mu, a_w0, a_w2, a_a0, a_a2, a_g2,
           a_k_k, a_k_a, a_r_k, a_lnx_w, a_lnx_b, b_conv_w, b_a_log, b_dt_bias, b_norm_w, c_bias, w_gu, w_down,
           norm_f):
    depth = w_in.shape[0]
    nb, seq, d = x_prompt.shape
    nd, tn, _ = x_sample.shape
    n_ada = w_ada.shape[2] // d

    r_all = -(-(nb + nd) // SUB) * SUB
    c_all = jnp.concatenate([c_prompt, c_sample, jnp.zeros((r_all - nb - nd, d), F32)], axis=0)
    mod = _ada(c_all, w_ada, b_ada)

    cache_kt = jnp.transpose(cache_k, (0, 1, 3, 4, 2))
    cache_vt = jnp.transpose(cache_v, (0, 1, 3, 4, 2))

    xp = x_prompt
    xs = x_sample.reshape(1, nd * tn, d)
    outs_p = [[] for _ in range(6)]
    outs_s = [[] for _ in range(6)]
    for l in range(depth):
        lp = dict(a_mu=a_mu[l], a_w0=a_w0[l], a_w2=a_w2[l], a_a0=a_a0[l], a_a2=a_a2[l], a_g2=a_g2[l],
                  a_k_k=a_k_k[l], a_k_a=a_k_a[l], a_r_k=a_r_k[l], a_lnx_w=a_lnx_w[l], a_lnx_b=a_lnx_b[l],
                  b_conv_w=b_conv_w[l], b_a_log=b_a_log[l], b_dt_bias=b_dt_bias[l], b_norm_w=b_norm_w[l])
        w_in_l = _reorder_w_in(w_in[l])
        w_out_l = w_out[l].astype(BF16)
        w_gu_l = w_gu[l].astype(BF16)
        w_down_l = w_down[l].astype(BF16)
        last = l == depth - 1

        mods_p = [m.reshape(nb, 1, d) for m in jnp.split(mod[l, 0:nb], n_ada, axis=-1)]
        mods_s = [jnp.repeat(m, tn, axis=0).reshape(1, nd * tn, d)
                  for m in jnp.split(mod[l, nb:nb + nd], n_ada, axis=-1)]

        sh1, sc1, gt1, sh2, sc2, gt2 = mods_p
        proj = _inproj(xp, norm1[l], sc1, sh1, w_in_l, tm=256)
        ya, wkv_p = _rwkv(proj, jnp.zeros((nb, 4 * A_WIDTH), F32),
                          jnp.zeros((nb, A_HEADS, A_DIM, A_DIM), F32), lp, c=64)
        yb, gdn_p = _gdn(proj, jnp.zeros((nb, B_CONV - 1, B_QKV), F32),
                         jnp.zeros((nb, B_HEADS, B_DIM, B_DIM), F32), lp, c=64)
        yc = _sb_prompt(proj, c_bias[l], tq=512, tk=128)
        xp = _tail(xp, ya, yb, yc, gt1, sh2, sc2, gt2, norm2[l], norm_f, w_out_l, w_gu_l, w_down_l,
                   tm=512, tf=256, final_norm=last)
        k_new = proj[:, :, COL_C + C_WIDTH:COL_C + 2 * C_WIDTH].reshape(nb, seq, C_HEADS, C_DIM)
        v_new = proj[:, :, COL_C + 2 * C_WIDTH:COL_C + 3 * C_WIDTH].reshape(nb, seq, C_HEADS, C_DIM)
        shift_new = proj[:, seq - 1, COL_A:COL_A + 4 * A_WIDTH]
        conv_new = proj[:, seq - (B_CONV - 1):, COL_BQKV:COL_BQKV + B_QKV]
        for lst, val in zip(outs_p, (k_new, v_new, shift_new, wkv_p, conv_new, gdn_p)):
            lst.append(val)

        sh1, sc1, gt1, sh2, sc2, gt2 = mods_s
        proj = _inproj(xs, norm1[l], sc1, sh1, w_in_l, tm=nd * tn).reshape(nd, tn, NP)
        ya, wkv_s = _rwkv(proj, state_shift[l], state_wkv[l], lp, c=SUB)
        yb, gdn_s = _gdn(proj, state_conv[l], state_gdn[l], lp, c=SUB)
        yc = _sb_sample(proj, cache_kt, cache_vt, page_table, c_bias[l], l, npage=8)
        flat = lambda a: a.reshape(1, nd * tn, a.shape[-1])
        xs = _tail(xs, flat(ya), flat(yb), flat(yc), gt1, sh2, sc2, gt2, norm2[l], norm_f, w_out_l, w_gu_l,
                   w_down_l, tm=nd * tn, tf=256, final_norm=last)
        k_new = proj[:, :, COL_C + C_WIDTH:COL_C + 2 * C_WIDTH].reshape(nd, tn, C_HEADS, C_DIM)
        v_new = proj[:, :, COL_C + 2 * C_WIDTH:COL_C + 3 * C_WIDTH].reshape(nd, tn, C_HEADS, C_DIM)
        shift_new = proj[:, tn - 1, COL_A:COL_A + 4 * A_WIDTH]
        conv_new = jnp.concatenate([state_conv[l], proj[:, :, COL_BQKV:COL_BQKV + B_QKV]],
                                   axis=1)[:, -(B_CONV - 1):]
        for lst, val in zip(outs_s, (k_new, v_new, shift_new, wkv_s, conv_new, gdn_s)):
            lst.append(val)

    y_prompt = xp
    y_sample = xs.reshape(nd, tn, d)
    p_out = tuple(jnp.stack(t) for t in outs_p)
    s_out = tuple(jnp.stack(t) for t in outs_s)
    return (y_prompt, y_sample) + p_out + s_out
```

```python
import functools

import jax
import jax.numpy as jnp
from jax import lax
from jax.experimental import pallas as pl
from jax.experimental.pallas import tpu as pltpu

F32 = jnp.float32
BF16 = jnp.bfloat16

NORM_EPS = 1e-6
L2_EPS = 1e-6
A_LNX_EPS = 64e-5

A_HEADS, A_DIM = 4, 64
A_WIDTH = A_HEADS * A_DIM
B_HEADS, B_DIM = 4, 128
B_WIDTH = B_HEADS * B_DIM
B_QKV = 3 * B_WIDTH
B_CONV = 4
C_HEADS, C_DIM = 4, 64
C_WIDTH = C_HEADS * C_DIM

NP = 4096
COL_A = 0
COL_BZ = 1024
COL_BQKV = 1536
COL_C = 3072
COL_BG = 3840

LANE = 128
SUB = 8
VMEM_LIMIT = 56 * 1024 * 1024

_NN = (((1,), (0,)), ((), ()))
_NT = (((1,), (1,)), ((), ()))
_TN = (((0,), (0,)), ((), ()))


def _dot(a, b):
    return lax.dot_general(a, b, _NN, preferred_element_type=F32)


def _dot_nt(a, b):
    return lax.dot_general(a, b, _NT, preferred_element_type=F32)


def _dot_tn(a, b):
    return lax.dot_general(a, b, _TN, preferred_element_type=F32)


def _iota2(shape, dim):
    return lax.broadcasted_iota(jnp.int32, shape, dim)


def _sigmoid(x):
    return jax.nn.sigmoid(x)


def _softplus(x):
    return jnp.maximum(x, 0.0) + jnp.log1p(jnp.exp(-jnp.abs(x)))


def _split2(x):
    hi = x.astype(BF16)
    return hi, (x - hi.astype(F32)).astype(BF16)


def _split3(x):
    hi = x.astype(BF16)
    r1 = x - hi.astype(F32)
    mid = r1.astype(BF16)
    lo = (r1 - mid.astype(F32)).astype(BF16)
    return hi, mid, lo


def _mm(a, b, dims=_NN):
    ah, al = a if isinstance(a, tuple) else _split2(a)
    bh, bl = b if isinstance(b, tuple) else _split2(b)
    dg = lambda x, y: lax.dot_general(x, y, dims, preferred_element_type=F32)
    return dg(ah, bh) + dg(ah, bl) + dg(al, bh)


def _mm01(x, e, dims=_NN, e_left=False):
    if e_left:
        dg = lambda p: lax.dot_general(e, p, dims, preferred_element_type=F32)
    else:
        dg = lambda p: lax.dot_general(p, e, dims, preferred_element_type=F32)
    hi, mid, lo = _split3(x)
    return dg(hi) + dg(mid) + dg(lo)


def _params(sem):
    return pltpu.CompilerParams(dimension_semantics=sem, vmem_limit_bytes=VMEM_LIMIT)


def _ada_kernel(c_ref, w_ref, b_ref, o_ref):
    c = c_ref[...]
    s = c * _sigmoid(c)
    o_ref[0] = _dot(s.astype(BF16), w_ref[0].astype(BF16)) + b_ref[0]


def _ada(c_all, w_ada, b_ada):
    depth, d, n = w_ada.shape
    r = c_all.shape[0]
    tn = 1024
    return pl.pallas_call(
        _ada_kernel,
        grid=(depth, n // tn),
        in_specs=[pl.BlockSpec((r, d), lambda l, j: (0, 0)),
                  pl.BlockSpec((1, d, tn), lambda l, j: (l, 0, j)),
                  pl.BlockSpec((1, 1, tn), lambda l, j: (l, 0, j))],
        out_specs=pl.BlockSpec((1, r, tn), lambda l, j: (l, 0, j)),
        out_shape=jax.ShapeDtypeStruct((depth, r, n), F32),
        compiler_params=_params(("parallel", "parallel")),
        name="ada",
    )(c_all, w_ada, b_ada.reshape(depth, 1, n))


def _rms_mod(x, g, sc, sh):
    ms = jnp.mean(x * x, axis=-1, keepdims=True)
    h = x * lax.rsqrt(ms + NORM_EPS) * g
    return h * (1.0 + sc) + sh


def _inproj_kernel(x_ref, g_ref, sc_ref, sh_ref, w_ref, o_ref, *, nchunk):
    h = _rms_mod(x_ref[0], g_ref[...], sc_ref[0], sh_ref[0]).astype(BF16)
    for n in range(NP // nchunk):
        o_ref[0, :, n * nchunk:(n + 1) * nchunk] = _dot(h, w_ref[:, n * nchunk:(n + 1) * nchunk])


def _mod_spec(mod, tm, d, ngrid):
    per_row = mod.shape[1] != 1
    if ngrid == 2:
        imap = (lambda b, i: (b, i, 0)) if per_row else (lambda b, i: (b, 0, 0))
    else:
        imap = (lambda b, i, f: (b, i, 0)) if per_row else (lambda b, i, f: (b, 0, 0))
    return pl.BlockSpec((1, tm if per_row else 1, d), imap)


def _inproj(x, g, sc, sh, w, tm):
    bn, t, d = x.shape
    return pl.pallas_call(
        functools.partial(_inproj_kernel, nchunk=1024),
        grid=(bn, t // tm),
        in_specs=[pl.BlockSpec((1, tm, d), lambda b, i: (b, i, 0)),
                  pl.BlockSpec((1, d), lambda b, i: (0, 0)),
                  _mod_spec(sc, tm, d, 2),
                  _mod_spec(sh, tm, d, 2),
                  pl.BlockSpec((d, NP), lambda b, i: (0, 0))],
        out_specs=pl.BlockSpec((1, tm, NP), lambda b, i: (b, i, 0)),
        out_shape=jax.ShapeDtypeStruct((bn, t, NP), F32),
        compiler_params=_params(("parallel", "parallel")),
        name="inproj",
    )(x, g.reshape(1, d), sc, sh, w)


def _tail_kernel(x_ref, ya_ref, yb_ref, yc_ref, gt1_ref, sh2_ref, sc2_ref, gt2_ref, g2_ref, gf_ref,
                 wo_ref, wg_ref, wu_ref, wd_ref, o_ref, x1_s, h_s, acc_s, *, nf, final_norm):
    f = pl.program_id(2)

    @pl.when(f == 0)
    def _():
        mix = (_dot(ya_ref[0].astype(BF16), wo_ref[0:A_WIDTH])
               + _dot(yb_ref[0].astype(BF16), wo_ref[A_WIDTH:A_WIDTH + B_WIDTH])
               + _dot(yc_ref[0].astype(BF16), wo_ref[A_WIDTH + B_WIDTH:A_WIDTH + B_WIDTH + C_WIDTH]))
        x1 = x_ref[0] + gt1_ref[0] * mix
        x1_s[...] = x1
        h_s[...] = _rms_mod(x1, g2_ref[...], sc2_ref[0], sh2_ref[0]).astype(BF16)
        acc_s[...] = jnp.zeros_like(acc_s)

    hb = h_s[...]
    gate = _dot(hb, wg_ref[...])
    up = _dot(hb, wu_ref[...])
    act = (gate * _sigmoid(gate)) * up
    acc_s[...] += _dot(act.astype(BF16), wd_ref[...])

    @pl.when(f == nf - 1)
    def _():
        y = x1_s[...] + gt2_ref[0] * acc_s[...]
        if final_norm:
            ms = jnp.mean(y * y, axis=-1, keepdims=True)
            y = y * lax.rsqrt(ms + NORM_EPS) * gf_ref[...]
        o_ref[0] = y


def _tail(x, ya, yb, yc, gt1, sh2, sc2, gt2, g2, gf, wo, wgu, wd, tm, tf, final_norm):
    bn, t, d = x.shape
    dff = wd.shape[0]
    nf = dff // tf
    row = lambda w: pl.BlockSpec((1, tm, w), lambda b, i, f: (b, i, 0))
    vec = pl.BlockSpec((1, d), lambda b, i, f: (0, 0))
    return pl.pallas_call(
        functools.partial(_tail_kernel, nf=nf, final_norm=final_norm),
        grid=(bn, t // tm, nf),
        in_specs=[row(d), row(A_WIDTH), row(B_WIDTH), row(C_WIDTH),
                  _mod_spec(gt1, tm, d, 3), _mod_spec(sh2, tm, d, 3), _mod_spec(sc2, tm, d, 3),
                  _mod_spec(gt2, tm, d, 3), vec, vec,
                  pl.BlockSpec((d, d), lambda b, i, f: (0, 0)),
                  pl.BlockSpec((d, tf), lambda b, i, f: (0, f)),
                  pl.BlockSpec((d, tf), lambda b, i, f: (0, f + nf)),
                  pl.BlockSpec((tf, d), lambda b, i, f: (f, 0))],
        out_specs=pl.BlockSpec((1, tm, d), lambda b, i, f: (b, i, 0)),
        out_shape=jax.ShapeDtypeStruct((bn, t, d), F32),
        scratch_shapes=[pltpu.VMEM((tm, d), F32), pltpu.VMEM((tm, d), BF16), pltpu.VMEM((tm, d), F32)],
        compiler_params=_params(("parallel", "parallel", "arbitrary")),
        name="tail",
    )(x, ya, yb, yc, gt1, sh2, sc2, gt2, g2.reshape(1, d), gf.reshape(1, d), wo, wgu, wgu, wd)


def _unit_lower_inverses(mats, nil):
    n = mats[0].shape[0]
    eye = (_iota2((n, n), 0) == _iota2((n, n), 1)).astype(F32)
    p = [-a for a in mats]
    t = [eye + m for m in p]
    k = 2
    while k < nil:
        ps = [_split2(x) for x in p]
        p = [_mm(x, x) for x in ps]
        t = [x + _mm(x, y) for x, y in zip(t, p)]
        k *= 2
    return t


def _tri_masks(c):
    rows = _iota2((c, c), 0)
    cols = _iota2((c, c), 1)
    return rows >= cols, rows > cols, rows <= cols


def _block_tri_masks(nh, c):
    n = nh * c
    rows = _iota2((n, n), 0)
    cols = _iota2((n, n), 1)
    same = (rows // c) == (cols // c)
    return jnp.logical_and(same, rows >= cols), jnp.logical_and(same, rows > cols)


def _head_blocks(x, nh):
    t, n = x.shape
    w = n // nh
    tiled = jnp.concatenate([x] * nh, axis=0)
    keep = (_iota2((nh * t, n), 0) // t) == (_iota2((nh * t, n), 1) // w)
    return jnp.where(keep, tiled, jnp.zeros_like(tiled))


def _sum_row_blocks(x, nh):
    t = x.shape[0] // nh
    out = x[0:t]
    for h in range(1, nh):
        out = out + x[h * t:(h + 1) * t]
    return out


def _scan_chunks(lg_ref, bh_ref, st_s, y_s, c, ncb, heads, dv):
    s = [st_s[h] for h in range(heads)]
    for ck in range(ncb):
        for h in range(heads):
            z = _mm(lg_ref[0, ck, h], s[h]) + bh_ref[0, ck, h]
            y_s[ck * c:(ck + 1) * c, h * dv:(h + 1) * dv] = z[0:c]
            s[h] = z[c:]
    for h in range(heads):
        st_s[h] = s[h]


def _scan_specs(heads, c, dk, dv, ncb):
    return [pl.BlockSpec((1, ncb, heads, c + dk, dk), lambda b, i: (b, i, 0, 0, 0)),
            pl.BlockSpec((1, ncb, heads, c + dk, dv), lambda b, i: (b, i, 0, 0, 0)),
            pl.BlockSpec((1, heads, dk, dv), lambda b, i: (b, 0, 0, 0))]


def _prev_rows(proj, t, rb, width, colblk):
    bn = proj.shape[0]
    if t == rb:
        return jnp.zeros((bn, SUB, width), F32), pl.BlockSpec((1, SUB, width), lambda b, i: (b, 0, 0))
    return proj, pl.BlockSpec((1, SUB, width),
                              lambda b, i: (b, jnp.maximum(i * (rb // SUB) - 1, 0), colblk))


def _rwkv_a_kernel(p_ref, pp_ref, sh0_ref, mu_ref, w0_ref, w2_ref, a0_ref, a2_ref, g2_ref, kk_ref, ka_ref,
                   rk_ref, lg_ref, bh_ref, eg_ref, ext_s, *, tb, c, nck):
    i = pl.program_id(1)
    nw = A_WIDTH
    rc = nck * c
    rb = nck * tb

    prev = jnp.where(i == 0, sh0_ref[0], pp_ref[0, SUB - 1:SUB, :])
    ext_s[SUB - 1:SUB] = prev
    ext_s[SUB:SUB + rb] = p_ref[0]
    if tb < c:
        ext_s[SUB + rb:SUB + rc] = jnp.zeros((rc - rb, ext_s.shape[1]), F32)
    p = ext_s[SUB:SUB + rc]
    p_prev = ext_s[SUB - 1:SUB - 1 + rc]
    m = p + (p_prev - p) * mu_ref[...]

    r = m[:, 0:nw]
    k = m[:, nw:2 * nw]
    v = m[:, 2 * nw:3 * nw]
    pw = m[:, 3 * nw:3 * nw + 64]
    pa = m[:, 3 * nw + 64:3 * nw + 128]
    pg = m[:, 3 * nw + 128:3 * nw + 256]

    w = -_softplus(-(w0_ref[...] + _mm(jnp.tanh(pw), w2_ref[...]))) - 0.5
    logd = -jnp.exp(w)
    a = _sigmoid(a0_ref[...] + _mm(pa, a2_ref[...]))
    g = _mm(_sigmoid(pg), g2_ref[...])

    ones_blk = ((_iota2((nw, nw), 0) // A_DIM) == (_iota2((nw, nw), 1) // A_DIM)).astype(BF16)
    hsum = lambda t: _mm01(t, ones_blk)

    kkf = k * kk_ref[...]
    kk = kkf / jnp.maximum(jnp.sqrt(hsum(kkf * kkf)), 1e-12)
    k2 = k * (1.0 + (a - 1.0) * ka_ref[...])

    if tb < c:
        valid = _iota2((rc, nw), 0) < tb
        zero = jnp.zeros((rc, nw), F32)
        logd = jnp.where(valid, logd, zero)
        kk = jnp.where(valid, kk, zero)
        k2 = jnp.where(valid, k2, zero)
        v = jnp.where(valid, v, zero)

    bonus = hsum(r * k2 * rk_ref[...]) * v
    eg_ref[0, :, 0:nw] = bonus[0:rb]
    eg_ref[0, :, nw:2 * nw] = g[0:rb]

    lincl = _tri_masks(c)[0].astype(BF16)
    nh = A_HEADS
    hc = nh * c
    incl, strict = _block_tri_masks(nh, c)
    zb = jnp.zeros((hc, hc), F32)
    eye = (_iota2((nw, nw), 0) == _iota2((nw, nw), 1)).astype(F32)
    each = lambda f, *lists: [f(*xs) for xs in zip(*lists)]
    rs = [slice(ck * c, (ck + 1) * c) for ck in range(nck)]
    logd_c = [logd[s] for s in rs]
    cum = each(lambda x: _mm01(x, lincl, e_left=True), logd_c)
    e_in = each(jnp.exp, cum)
    e_ex = each(lambda x, y: jnp.exp(x - y), cum, logd_c)
    e_neg = each(lambda x: jnp.exp(-x), cum)
    pc = [x[c - 1:c] for x in e_in]
    kkt = each(lambda s, e: _head_blocks(kk[s] * e, nh), rs, e_ex)
    rt = each(lambda s, e: _head_blocks(r[s] * e, nh), rs, e_in)
    kah = each(lambda s, e: _head_blocks(kk[s] * a[s] * e, nh), rs, e_neg)
    kh = each(lambda s, e: _head_blocks(k2[s] * e, nh), rs, e_neg)
    v_s = each(lambda s: _split2(_head_blocks(v[s], nh)), rs)
    pr = each(lambda x1, x2, y1, y2: _mm(jnp.concatenate([x1, x2], axis=0), jnp.concatenate([y1, y2], axis=0),
                                         _NT), kkt, rt, kah, kh)
    la = [jnp.where(strict, x[0:hc, 0:hc], zb) for x in pr]
    lk = [jnp.where(strict, x[0:hc, hc:2 * hc], zb) for x in pr]
    ma = [jnp.where(incl, x[hc:2 * hc, 0:hc], zb) for x in pr]
    mk = [jnp.where(incl, x[hc:2 * hc, hc:2 * hc], zb) for x in pr]
    tinv = _unit_lower_inverses(la, c)
    lkv = each(_mm, lk, v_s)
    w1b1 = each(lambda t, x, y: _split2(-_mm(t, jnp.concatenate([x, y], axis=1))), tinv, kkt, lkv)
    mw = each(_mm, ma, w1b1)
    mkv = each(_mm, mk, v_s)
    gh = each(lambda x, p, y: _mm(x * p, y, _TN), kah, pc, w1b1)
    khv = each(lambda x, p, y: _mm(x * p, y, _TN), kh, pc, v_s)
    for ck in range(nck):
        lg_ref[0, ck, 0:c] = _sum_row_blocks(rt[ck] + mw[ck][:, 0:nw], nh)
        lg_ref[0, ck, c:c + A_DIM] = _sum_row_blocks(eye * pc[ck] + gh[ck][:, 0:nw], nh)
        bh_ref[0, ck, 0:c] = _sum_row_blocks(mw[ck][:, nw:2 * nw] + mkv[ck], nh)
        bh_ref[0, ck, c:c + A_DIM] = _sum_row_blocks(gh[ck][:, nw:2 * nw] + khv[ck], nh)


def _rwkv_b_kernel(lg_ref, bh_ref, s0_ref, eg_ref, lw_ref, lb_ref, y_ref, sn_ref, st_s, y_s, *, tb, c, ncb):
    nw = A_WIDTH
    nh = A_HEADS

    @pl.when(pl.program_id(1) == 0)
    def _():
        st_s[...] = _head_blocks(s0_ref[0], nh)

    s = st_s[...]
    rbk = c + A_DIM
    for ck in range(ncb):
        z = _mm(_head_blocks(lg_ref[0, ck], nh), s)
        y_c = bh_ref[0, ck, 0:c]
        for h in range(nh):
            y_c = y_c + z[h * rbk:h * rbk + c]
        y_s[ck * c:(ck + 1) * c] = y_c
        s = (jnp.concatenate([z[h * rbk + c:(h + 1) * rbk] for h in range(nh)], axis=0)
             + _head_blocks(bh_ref[0, ck, c:rbk], nh))
    st_s[...] = s
    rb = ncb * tb
    y = y_s[...]
    ones_blk = ((_iota2((nw, nw), 0) // A_DIM) == (_iota2((nw, nw), 1) // A_DIM)).astype(BF16)
    hsum = lambda t: _mm01(t, ones_blk)
    mean = hsum(y) * (1.0 / A_DIM)
    yc = y - mean
    var = hsum(yc * yc) * (1.0 / A_DIM)
    yn = yc * lax.rsqrt(var + A_LNX_EPS) * lw_ref[...] + lb_ref[...]
    eg = eg_ref[0]
    y_ref[0] = (yn[0:rb] + eg[:, 0:nw]) * eg[:, nw:2 * nw]
    sn_ref[0] = _sum_row_blocks(s, nh)


def _rwkv(proj, shift0, s0, lp, c, nck, ncb):
    bn, t, _ = proj.shape
    tb = min(t, c)
    nck = nck if tb == c else 1
    ncb = ncb if tb == c else 1
    nc = t // tb
    cols = 4 * A_WIDTH
    rb = nck * tb
    vec = lambda n: pl.BlockSpec((1, n), lambda b, i: (0, 0))
    mat = lambda r, n: pl.BlockSpec((r, n), lambda b, i: (0, 0))
    row = lambda a: a.reshape(1, -1)
    pp, pp_spec = _prev_rows(proj, t, rb, cols, COL_A // cols)
    lg_shape = jax.ShapeDtypeStruct((bn, nc, c + A_DIM, A_WIDTH), F32)
    lg_spec = pl.BlockSpec((1, nck, c + A_DIM, A_WIDTH), lambda b, i: (b, i, 0, 0))
    lg, bh, eg = pl.pallas_call(
        functools.partial(_rwkv_a_kernel, tb=tb, c=c, nck=nck),
        grid=(bn, nc // nck),
        in_specs=[pl.BlockSpec((1, rb, cols), lambda b, i: (b, i, COL_A // cols)),
                  pp_spec,
                  pl.BlockSpec((1, 1, cols), lambda b, i: (b, 0, 0)),
                  vec(cols), vec(A_WIDTH), mat(64, A_WIDTH), vec(A_WIDTH), mat(64, A_WIDTH), mat(128, A_WIDTH),
                  vec(A_WIDTH), vec(A_WIDTH), vec(A_WIDTH)],
        out_specs=[lg_spec, lg_spec, pl.BlockSpec((1, rb, 2 * A_WIDTH), lambda b, i: (b, i, 0))],
        out_shape=[lg_shape, lg_shape, jax.ShapeDtypeStruct((bn, t, 2 * A_WIDTH), F32)],
        scratch_shapes=[pltpu.VMEM((SUB + nck * c, cols), F32)],
        compiler_params=_params(("parallel", "parallel")),
        name="rwkv_a",
    )(proj, pp, shift0.reshape(bn, 1, cols), row(lp["a_mu"]), row(lp["a_w0"]), lp["a_w2"], row(lp["a_a0"]),
      lp["a_a2"], lp["a_g2"], row(lp["a_k_k"]), row(lp["a_k_a"]), row(lp["a_r_k"]))
    y, st = pl.pallas_call(
        functools.partial(_rwkv_b_kernel, tb=tb, c=c, ncb=ncb),
        grid=(bn, nc // ncb),
        in_specs=[pl.BlockSpec((1, ncb, c + A_DIM, A_WIDTH), lambda b, i: (b, i, 0, 0)),
                  pl.BlockSpec((1, ncb, c + A_DIM, A_WIDTH), lambda b, i: (b, i, 0, 0)),
                  pl.BlockSpec((1, A_DIM, A_WIDTH), lambda b, i: (b, 0, 0)),
                  pl.BlockSpec((1, ncb * tb, 2 * A_WIDTH), lambda b, i: (b, i, 0)), vec(A_WIDTH), vec(A_WIDTH)],
        out_specs=[pl.BlockSpec((1, ncb * tb, A_WIDTH), lambda b, i: (b, i, 0)),
                   pl.BlockSpec((1, A_DIM, A_WIDTH), lambda b, i: (b, 0, 0))],
        out_shape=[jax.ShapeDtypeStruct((bn, t, A_WIDTH), F32),
                   jax.ShapeDtypeStruct((bn, A_DIM, A_WIDTH), F32)],
        scratch_shapes=[pltpu.VMEM((A_WIDTH, A_WIDTH), F32), pltpu.VMEM((ncb * c, A_WIDTH), F32)],
        compiler_params=_params(("parallel", "arbitrary")),
        name="rwkv_b",
    )(lg, bh, jnp.transpose(s0, (0, 3, 1, 2)).reshape(bn, A_DIM, A_WIDTH), eg,
      row(lp["a_lnx_w"]), row(lp["a_lnx_b"]))
    return y, jnp.transpose(st.reshape(bn, A_DIM, A_HEADS, A_DIM), (0, 2, 3, 1))


def _gdn_a_kernel(qkv_ref, qp_ref, bg_ref, cv0_ref, cw_ref, alog_ref, dtb_ref, lg_ref, bh_ref, ext_s, bg_s,
                  *, tb, c, nck):
    i = pl.program_id(1)
    rc = nck * c
    rb = nck * tb

    ext_s[0:SUB] = jnp.where(i == 0, cv0_ref[0], qp_ref[0])
    ext_s[SUB:SUB + rb] = qkv_ref[0]
    if tb < c:
        ext_s[SUB + rb:SUB + rc] = jnp.zeros((rc - rb, ext_s.shape[1]), F32)
    cw = cw_ref[...]
    u = ext_s[SUB - 3:SUB - 3 + rc] * cw[0:1]
    for j in range(1, B_CONV):
        u = u + ext_s[SUB - 3 + j:SUB - 3 + j + rc] * cw[j:j + 1]
    qkv = u * _sigmoid(u)

    nb = bg_s.shape[1]
    bg_s[0:rb] = bg_ref[0]
    if tb < c:
        bg_s[rb:rc] = jnp.zeros((rc - rb, nb), F32)
    bg = bg_s[...]
    beta_all = _sigmoid(bg)
    gl = -jnp.exp(alog_ref[...]) * _softplus(bg + dtb_ref[...])
    if tb < c:
        valid = _iota2((rc, nb), 0) < tb
        beta_all = jnp.where(valid, beta_all, jnp.zeros_like(beta_all))
        gl = jnp.where(valid, gl, jnp.zeros_like(gl))
    gl = gl[:, 0:LANE]

    lincl, _, upper = _tri_masks(c)
    nh = B_HEADS
    hc = nh * c
    incl, strict = _block_tri_masks(nh, c)
    zb = jnp.zeros((hc, hc), F32)
    own_head = (_iota2((hc, B_WIDTH), 0) // c) == (_iota2((hc, B_WIDTH), 1) // B_DIM)
    eye_rows = (_iota2((B_WIDTH, B_DIM), 0) % B_DIM) == _iota2((B_WIDTH, B_DIM), 1)
    each = lambda f, *lists: [f(*xs) for xs in zip(*lists)]
    rs = [slice(ck * c, (ck + 1) * c) for ck in range(nck)]
    heads_on_rows = lambda s, col0: jnp.concatenate(
        [qkv[s, col0 + h * B_DIM:col0 + (h + 1) * B_DIM] for h in range(nh)], axis=0)
    l2n = lambda x: x * lax.rsqrt(jnp.sum(x * x, axis=-1, keepdims=True) + L2_EPS)
    gc = each(lambda s: _mm01(gl[s], lincl.astype(BF16), e_left=True), rs)
    gct = each(lambda s: _mm01(gl[s], upper.astype(BF16), _TN), rs)
    q_r = each(lambda s: l2n(heads_on_rows(s, 0)) * (B_DIM ** -0.5), rs)
    k_r = each(lambda s: l2n(heads_on_rows(s, B_WIDTH)), rs)
    v_r = each(lambda s: heads_on_rows(s, 2 * B_WIDTH), rs)
    gcol = [jnp.concatenate([x[:, nh + h:nh + h + 1] for h in range(nh)], axis=0) for x in gc]
    grow = [jnp.concatenate([x[nh + h:nh + h + 1, :] for h in range(nh)], axis=1) for x in gct]
    bcol = [jnp.concatenate([beta_all[s, h:h + 1] for h in range(nh)], axis=0) for s in rs]
    glast = [jnp.concatenate([jnp.broadcast_to(x[c - 1:c, nh + h:nh + h + 1], (c, 1)) for h in range(nh)],
                             axis=0) for x in gc]
    eglast = [jnp.concatenate([jnp.broadcast_to(jnp.exp(x[c - 1:c, nh + h:nh + h + 1]), (B_DIM, 1))
                               for h in range(nh)], axis=0) for x in gc]
    decay = each(lambda x, y: jnp.where(incl, jnp.exp(jnp.where(incl, x - y, zb)), zb), gcol, grow)
    kb = each(lambda x, y: x * y, k_r, bcol)
    k_s = each(_split2, k_r)
    amat = each(lambda x, y, d: jnp.where(strict, _mm(x, y, _NT) * d, zb), kb, k_s, decay)
    qk = each(lambda x, y, d: _mm(x, y, _NT) * d, q_r, k_s, decay)
    tinv = _unit_lower_inverses(amat, c)
    eg = each(jnp.exp, gcol)
    uu = each(lambda t, x, b, y, e: _split2(_mm(t, jnp.concatenate([x * b, y * e], axis=1))),
              tinv, v_r, bcol, kb, eg)
    qu = each(_mm, qk, uu)
    kd_blk = each(lambda x, gl_, gc_: jnp.where(own_head, jnp.concatenate([x * jnp.exp(gl_ - gc_)] * nh, axis=1),
                                                jnp.zeros((hc, B_WIDTH), F32)), k_r, glast, gcol)
    ku = each(lambda x, y: _mm(x, y, _TN), kd_blk, uu)
    for ck in range(nck):
        w2 = q_r[ck] * eg[ck] - qu[ck][:, B_DIM:]
        g = jnp.where(eye_rows, eglast[ck], jnp.zeros((B_WIDTH, B_DIM), F32)) - ku[ck][:, B_DIM:]
        for h in range(nh):
            lg_ref[0, ck, h, 0:c] = w2[h * c:(h + 1) * c]
            lg_ref[0, ck, h, c:c + B_DIM] = g[h * B_DIM:(h + 1) * B_DIM]
            bh_ref[0, ck, h, 0:c] = qu[ck][h * c:(h + 1) * c, 0:B_DIM]
            bh_ref[0, ck, h, c:c + B_DIM] = ku[ck][h * B_DIM:(h + 1) * B_DIM, 0:B_DIM]


def _gdn_b_kernel(lg_ref, bh_ref, s0_ref, z_ref, nw_ref, y_ref, sn_ref, st_s, y_s, *, tb, c, ncb):
    @pl.when(pl.program_id(1) == 0)
    def _():
        st_s[...] = s0_ref[0]

    _scan_chunks(lg_ref, bh_ref, st_s, y_s, c, ncb, B_HEADS, B_DIM)
    rb = ncb * tb
    z = z_ref[0]
    for h in range(B_HEADS):
        sl = slice(h * B_DIM, (h + 1) * B_DIM)
        o = y_s[:, sl]
        on = o * lax.rsqrt(jnp.mean(o * o, axis=-1, keepdims=True) + NORM_EPS) * nw_ref[...]
        z_h = z[:, sl]
        y_ref[0, :, sl] = on[0:rb] * (z_h * _sigmoid(z_h))
    sn_ref[0] = st_s[...]


def _gdn(proj, conv0, s0, lp, c, nck, ncb):
    bn, t, _ = proj.shape
    tb = min(t, c)
    nck = nck if tb == c else 1
    ncb = ncb if tb == c else 1
    nc = t // tb
    rb = nck * tb
    nb = NP - COL_BG
    cv0 = jnp.concatenate([jnp.zeros((bn, SUB - (B_CONV - 1), B_QKV), F32), conv0], axis=1)
    lane8 = lambda a: jnp.zeros((1, nb), F32).at[0, B_HEADS:2 * B_HEADS].set(a)
    qp, qp_spec = _prev_rows(proj, t, rb, B_QKV, COL_BQKV // B_QKV)
    lg_shape = jax.ShapeDtypeStruct((bn, nc, B_HEADS, c + B_DIM, B_DIM), F32)
    lg_spec = pl.BlockSpec((1, nck, B_HEADS, c + B_DIM, B_DIM), lambda b, i: (b, i, 0, 0, 0))
    lg, bh = pl.pallas_call(
        functools.partial(_gdn_a_kernel, tb=tb, c=c, nck=nck),
        grid=(bn, nc // nck),
        in_specs=[pl.BlockSpec((1, rb, B_QKV), lambda b, i: (b, i, COL_BQKV // B_QKV)),
                  qp_spec,
                  pl.BlockSpec((1, rb, nb), lambda b, i: (b, i, COL_BG // nb)),
                  pl.BlockSpec((1, SUB, B_QKV), lambda b, i: (b, 0, 0)),
                  pl.BlockSpec((B_CONV, B_QKV), lambda b, i: (0, 0)),
                  pl.BlockSpec((1, nb), lambda b, i: (0, 0)),
                  pl.BlockSpec((1, nb), lambda b, i: (0, 0))],
        out_specs=[lg_spec, lg_spec],
        out_shape=[lg_shape, lg_shape],
        scratch_shapes=[pltpu.VMEM((SUB + nck * c, B_QKV), F32), pltpu.VMEM((nck * c, nb), F32)],
        compiler_params=_params(("parallel", "parallel")),
        name="gdn_a",
    )(proj, qp, proj, cv0, lp["b_conv_w"], lane8(lp["b_a_log"]), lane8(lp["b_dt_bias"]))
    return pl.pallas_call(
        functools.partial(_gdn_b_kernel, tb=tb, c=c, ncb=ncb),
        grid=(bn, nc // ncb),
        in_specs=_scan_specs(B_HEADS, c, B_DIM, B_DIM, ncb)
                 + [pl.BlockSpec((1, ncb * tb, B_WIDTH), lambda b, i: (b, i, COL_BZ // B_WIDTH)),
                    pl.BlockSpec((1, B_DIM), lambda b, i: (0, 0))],
        out_specs=[pl.BlockSpec((1, ncb * tb, B_WIDTH), lambda b, i: (b, i, 0)),
                   pl.BlockSpec((1, B_HEADS, B_DIM, B_DIM), lambda b, i: (b, 0, 0, 0))],
        out_shape=[jax.ShapeDtypeStruct((bn, t, B_WIDTH), F32),
                   jax.ShapeDtypeStruct((bn, B_HEADS, B_DIM, B_DIM), F32)],
        scratch_shapes=[pltpu.VMEM((B_HEADS, B_DIM, B_DIM), F32), pltpu.VMEM((ncb * c, B_WIDTH), F32)],
        compiler_params=_params(("parallel", "arbitrary")),
        name="gdn_b",
    )(lg, bh, s0, proj, lp["b_norm_w"].reshape(1, B_DIM))


def _sbp_kernel(bias_ref, q_ref, k_ref, v_ref, o_ref, carry_s, acc_s, *, tq, tk):
    qi = pl.program_id(1)
    j = pl.program_id(2)
    nsub = tq // tk

    @pl.when(j == 0)
    def _():
        carry_s[...] = jnp.zeros_like(carry_s)
        acc_s[...] = jnp.zeros_like(acc_s)

    def sweep(diag):
        qb = (q_ref[0] * (C_DIM ** -0.5)).astype(BF16)
        kb = k_ref[0].astype(BF16)
        vb = v_ref[0].astype(BF16)
        tri = (_iota2((tk, tk), 1) >= _iota2((tk, tk), 0)).astype(BF16)
        hsl = lambda h: slice(h * C_DIM, (h + 1) * C_DIM)
        tiles = [(s, h) for s in reversed(range(nsub)) for h in range(C_HEADS)]
        valid = {s: (s * tk + _iota2((tk, tq), 0)) < _iota2((tk, tq), 1) if diag else None for s in range(nsub)}
        zt = {(s, h): _dot_nt(kb[s * tk:(s + 1) * tk, hsl(h)], qb[:, hsl(h)]) + bias_ref[h] for s, h in tiles}
        rc = {}
        for s, h in tiles:
            z = zt[s, h]
            sp = jnp.maximum(z, 0.0) + jnp.log(1.0 + jnp.exp(-jnp.abs(z)))
            if diag:
                sp = jnp.where(valid[s], sp, jnp.zeros_like(sp))
            hi, lo = _split2(sp)
            rc[s, h] = _dot(tri, hi) + _dot(tri, lo)
        carry = [carry_s[h:h + 1, :] for h in range(C_HEADS)]
        acc = [acc_s[h] for h in range(C_HEADS)]
        for s, h in tiles:
            at = jnp.exp(zt[s, h] - rc[s, h] - carry[h])
            if diag:
                at = jnp.where(valid[s], at, jnp.zeros_like(at))
            acc[h] = acc[h] + _dot_tn(vb[s * tk:(s + 1) * tk, hsl(h)], at.astype(BF16))
            carry[h] = carry[h] + rc[s, h][0:1]
        for h in range(C_HEADS):
            carry_s[h:h + 1, :] = carry[h]
            acc_s[h] = acc[h]

    @pl.when(j == 0)
    def _():
        sweep(True)

    @pl.when(jnp.logical_and(j > 0, j <= qi))
    def _():
        sweep(False)

    @pl.when(j == qi)
    def _():
        o_ref[0] = acc_s[...].reshape(C_WIDTH, tq).T


def _sb_prompt(proj, bias, tq, tk):
    bn, t, _ = proj.shape
    nq = t // tq
    qcol = COL_C // C_WIDTH
    return pl.pallas_call(
        functools.partial(_sbp_kernel, tq=tq, tk=tk),
        grid_spec=pltpu.PrefetchScalarGridSpec(
            num_scalar_prefetch=0,
            grid=(bn, nq, nq),
            in_specs=[pl.BlockSpec(memory_space=pltpu.SMEM),
                      pl.BlockSpec((1, tq, C_WIDTH), lambda b, i, j: (b, i, qcol)),
                      pl.BlockSpec((1, tq, C_WIDTH), lambda b, i, j: (b, jnp.maximum(i - j, 0), qcol + 1)),
                      pl.BlockSpec((1, tq, C_WIDTH), lambda b, i, j: (b, jnp.maximum(i - j, 0), qcol + 2))],
            out_specs=pl.BlockSpec((1, tq, C_WIDTH), lambda b, i, j: (b, i, 0)),
            scratch_shapes=[pltpu.VMEM((C_HEADS, tq), F32), pltpu.VMEM((C_HEADS, C_DIM, tq), F32)]),
        out_shape=jax.ShapeDtypeStruct((bn, t, C_WIDTH), F32),
        compiler_params=_params(("parallel", "parallel", "arbitrary")),
        name="sbp",
    )(bias, proj, proj, proj)


def _sbs_kernel(pt_ref, bias_ref, q_ref, k_ref, v_ref, *rest, tn, npage, ps):
    kp_refs = rest[0:npage]
    vp_refs = rest[npage:2 * npage]
    o_ref, q_s, carry_s, acc_s, kown_s, vown_s = rest[2 * npage:]
    j = pl.program_id(1)
    nj = pl.num_programs(1)
    rows = C_HEADS * SUB
    tri = (_iota2((ps, ps), 0) >= _iota2((ps, ps), 1)).astype(BF16)
    bias_col = jnp.concatenate([jnp.full((SUB, 1), bias_ref[h], F32) for h in range(C_HEADS)], axis=0)
    hsl = lambda h: slice(h * C_DIM, (h + 1) * C_DIM)
    rsl = lambda h: slice(h * SUB, (h + 1) * SUB)

    @pl.when(j == 0)
    def _():
        q_s[...] = jnp.zeros_like(q_s)
        q_s[0:tn] = q_ref[0] * (C_DIM ** -0.5)
        kown_s[...] = jnp.zeros_like(kown_s)
        vown_s[...] = jnp.zeros_like(vown_s)
        kown_s[0:tn] = k_ref[0]
        vown_s[0:tn] = v_ref[0]
        qb = q_s[...].astype(BF16)
        kb = kown_s[...].astype(BF16)
        vb = vown_s[...].astype(BF16)
        z = jnp.concatenate([_dot_nt(qb[:, hsl(h)], kb[:, hsl(h)]) for h in range(C_HEADS)], axis=0) + bias_col
        valid = _iota2((rows, ps), 1) < (_iota2((rows, ps), 0) % SUB)
        zero = jnp.zeros((rows, ps), F32)
        rc = _mm01(jnp.where(valid, _softplus(z), zero), tri)
        a = jnp.where(valid, jnp.exp(z - rc), zero)
        for h in range(C_HEADS):
            acc_s[h] = _dot(a[rsl(h)].astype(BF16), vb[:, hsl(h)])
        carry_s[...] = rc[:, 0:1]

    qb = q_s[...].astype(BF16)
    kt = [kp_refs[i][0, 0].astype(BF16) for i in range(npage)]
    vt = [vp_refs[i][0, 0].astype(BF16) for i in range(npage)]
    z = jnp.concatenate(
        [_dot(qb[:, hsl(h)], jnp.concatenate([kt[i][h] for i in range(npage)], axis=1)) for h in range(C_HEADS)],
        axis=0) + bias_col
    hi, mid, lo = _split3(_softplus(z))
    carry = carry_s[...]
    a_segs = []
    for i in range(npage):
        seg = slice(i * ps, (i + 1) * ps)
        rc = _dot(hi[:, seg], tri) + _dot(mid[:, seg], tri) + _dot(lo[:, seg], tri)
        a_segs.append(jnp.exp(z[:, seg] - rc - carry))
        carry = carry + rc[:, 0:1]
    carry_s[...] = carry
    a = jnp.concatenate(a_segs, axis=1)
    for h in range(C_HEADS):
        v_h = jnp.concatenate([vt[i][h] for i in range(npage)], axis=1)
        acc_s[h] += _dot_nt(a[rsl(h)].astype(BF16), v_h)

    @pl.when(j == nj - 1)
    def _():
        for h in range(C_HEADS):
            o_ref[0, :, hsl(h)] = acc_s[h][0:tn]


def _sb_sample(proj, cache_kt, cache_vt, page_table, bias, layer, npage):
    bn, tn, _ = proj.shape
    n_pages = page_table.shape[1]
    ps = cache_kt.shape[-1]
    qcol = COL_C // C_WIDTH
    page_spec = lambda i: pl.BlockSpec(
        (1, 1, C_HEADS, C_DIM, ps),
        lambda b, j, pt: (layer, pt[b, n_pages - 1 - (j * npage + i)], 0, 0, 0))
    own = lambda off: pl.BlockSpec((1, tn, C_WIDTH), lambda b, j, pt: (b, 0, qcol + off))
    return pl.pallas_call(
        functools.partial(_sbs_kernel, tn=tn, npage=npage, ps=ps),
        grid_spec=pltpu.PrefetchScalarGridSpec(
            num_scalar_prefetch=1,
            grid=(bn, n_pages // npage),
            in_specs=[pl.BlockSpec(memory_space=pltpu.SMEM), own(0), own(1), own(2)]
                     + [page_spec(i) for i in range(npage)] + [page_spec(i) for i in range(npage)],
            out_specs=pl.BlockSpec((1, tn, C_WIDTH), lambda b, j, pt: (b, 0, 0)),
            scratch_shapes=[pltpu.VMEM((SUB, C_WIDTH), F32),
                            pltpu.VMEM((C_HEADS * SUB, 1), F32),
                            pltpu.VMEM((C_HEADS, SUB, C_DIM), F32),
                            pltpu.VMEM((ps, C_WIDTH), F32),
                            pltpu.VMEM((ps, C_WIDTH), F32)]),
        out_shape=jax.ShapeDtypeStruct((bn, tn, C_WIDTH), F32),
        compiler_params=_params(("parallel", "arbitrary")),
        name="sbs",
    )(page_table, bias, proj, proj, proj, *([cache_kt] * npage), *([cache_vt] * npage))


def _reorder_w_in(w):
    a_cols = 4 * A_WIDTH
    b0 = a_cols
    bz0 = b0 + B_QKV
    bg0 = bz0 + B_WIDTH
    c0 = bg0 + 2 * B_HEADS
    c1 = c0 + 3 * C_WIDTH
    pad = jnp.zeros((w.shape[0], NP - (COL_BG + 2 * B_HEADS)), w.dtype)
    return jnp.concatenate([w[:, 0:a_cols], w[:, bz0:bg0], w[:, b0:bz0], w[:, c0:c1], w[:, bg0:c0], pad],
                           axis=1).astype(BF16)


def kernel(x_prompt, x_sample, cache_k, cache_v, state_shift, state_wkv, state_conv, state_gdn, page_table,
           c_prompt, c_sample, w_ada, b_ada, norm1, norm2, w_in, w_out, a_mu, a_w0, a_w2, a_a0, a_a2, a_g2,
           a_k_k, a_k_a, a_r_k, a_lnx_w, a_lnx_b, b_conv_w, b_a_log, b_dt_bias, b_norm_w, c_bias, w_gu, w_down,
           norm_f):
    depth = w_in.shape[0]
    nb, seq, d = x_prompt.shape
    nd, tn, _ = x_sample.shape
    n_ada = w_ada.shape[2] // d

    r_all = -(-(nb + nd) // SUB) * SUB
    c_all = jnp.concatenate([c_prompt, c_sample, jnp.zeros((r_all - nb - nd, d), F32)], axis=0)
    mod = _ada(c_all, w_ada, b_ada)

    cache_kt = jnp.transpose(cache_k, (0, 1, 3, 4, 2))
    cache_vt = jnp.transpose(cache_v, (0, 1, 3, 4, 2))

    xp = x_prompt
    xs = x_sample.reshape(1, nd * tn, d)
    outs_p = [[] for _ in range(6)]
    outs_s = [[] for _ in range(6)]
    for l in range(depth):
        lp = dict(a_mu=a_mu[l], a_w0=a_w0[l], a_w2=a_w2[l], a_a0=a_a0[l], a_a2=a_a2[l], a_g2=a_g2[l],
                  a_k_k=a_k_k[l], a_k_a=a_k_a[l], a_r_k=a_r_k[l], a_lnx_w=a_lnx_w[l], a_lnx_b=a_lnx_b[l],
                  b_conv_w=b_conv_w[l], b_a_log=b_a_log[l], b_dt_bias=b_dt_bias[l], b_norm_w=b_norm_w[l])
        w_in_l = _reorder_w_in(w_in[l])
        w_out_l = w_out[l].astype(BF16)
        w_gu_l = w_gu[l].astype(BF16)
        w_down_l = w_down[l].astype(BF16)
        last = l == depth - 1

        mods_p = [m.reshape(nb, 1, d) for m in jnp.split(mod[l, 0:nb], n_ada, axis=-1)]
        mods_s = [jnp.repeat(m, tn, axis=0).reshape(1, nd * tn, d)
                  for m in jnp.split(mod[l, nb:nb + nd], n_ada, axis=-1)]

        sh1, sc1, gt1, sh2, sc2, gt2 = mods_p
        proj = _inproj(xp, norm1[l], sc1, sh1, w_in_l, tm=256)
        ya, wkv_p = _rwkv(proj, jnp.zeros((nb, 4 * A_WIDTH), F32),
                          jnp.zeros((nb, A_HEADS, A_DIM, A_DIM), F32), lp, c=64, nck=4, ncb=8)
        yb, gdn_p = _gdn(proj, jnp.zeros((nb, B_CONV - 1, B_QKV), F32),
                         jnp.zeros((nb, B_HEADS, B_DIM, B_DIM), F32), lp, c=64, nck=4, ncb=8)
        yc = _sb_prompt(proj, c_bias[l], tq=512, tk=128)
        xp = _tail(xp, ya, yb, yc, gt1, sh2, sc2, gt2, norm2[l], norm_f, w_out_l, w_gu_l, w_down_l,
                   tm=512, tf=256, final_norm=last)
        k_new = proj[:, :, COL_C + C_WIDTH:COL_C + 2 * C_WIDTH].reshape(nb, seq, C_HEADS, C_DIM)
        v_new = proj[:, :, COL_C + 2 * C_WIDTH:COL_C + 3 * C_WIDTH].reshape(nb, seq, C_HEADS, C_DIM)
        shift_new = proj[:, seq - 1, COL_A:COL_A + 4 * A_WIDTH]
        conv_new = proj[:, seq - (B_CONV - 1):, COL_BQKV:COL_BQKV + B_QKV]
        for lst, val in zip(outs_p, (k_new, v_new, shift_new, wkv_p, conv_new, gdn_p)):
            lst.append(val)

        sh1, sc1, gt1, sh2, sc2, gt2 = mods_s
        proj = _inproj(xs, norm1[l], sc1, sh1, w_in_l, tm=nd * tn).reshape(nd, tn, NP)
        ya, wkv_s = _rwkv(proj, state_shift[l], state_wkv[l], lp, c=SUB, nck=1, ncb=1)
        yb, gdn_s = _gdn(proj, state_conv[l], state_gdn[l], lp, c=SUB, nck=1, ncb=1)
        yc = _sb_sample(proj, cache_kt, cache_vt, page_table, c_bias[l], l, npage=16)
        flat = lambda a: a.reshape(1, nd * tn, a.shape[-1])
        xs = _tail(xs, flat(ya), flat(yb), flat(yc), gt1, sh2, sc2, gt2, norm2[l], norm_f, w_out_l, w_gu_l,
                   w_down_l, tm=nd * tn, tf=256, final_norm=last)
        k_new = proj[:, :, COL_C + C_WIDTH:COL_C + 2 * C_WIDTH].reshape(nd, tn, C_HEADS, C_DIM)
        v_new = proj[:, :, COL_C + 2 * C_WIDTH:COL_C + 3 * C_WIDTH].reshape(nd, tn, C_HEADS, C_DIM)
        shift_new = proj[:, tn - 1, COL_A:COL_A + 4 * A_WIDTH]
        conv_new = jnp.concatenate([state_conv[l], proj[:, :, COL_BQKV:COL_BQKV + B_QKV]],
                                   axis=1)[:, -(B_CONV - 1):]
        for lst, val in zip(outs_s, (k_new, v_new, shift_new, wkv_s, conv_new, gdn_s)):
            lst.append(val)

    y_prompt = xp
    y_sample = xs.reshape(nd, tn, d)
    p_out = tuple(jnp.stack(t) for t in outs_p)
    s_out = tuple(jnp.stack(t) for t in outs_s)
    return (y_prompt, y_sample) + p_out + s_out
```

```python
import functools

import jax
import jax.numpy as jnp
from jax import lax
from jax.experimental import pallas as pl
from jax.experimental.pallas import tpu as pltpu

F32 = jnp.float32
BF16 = jnp.bfloat16

LOG2E = 1.4426950408889634
NORM_EPS = 1e-6
L2_EPS = 1e-6
A_LNX_EPS = 64e-5

A_HEADS, A_DIM = 4, 64
A_WIDTH = A_HEADS * A_DIM
B_HEADS, B_DIM = 4, 128
B_WIDTH = B_HEADS * B_DIM
B_QKV = 3 * B_WIDTH
B_CONV = 4
C_HEADS, C_DIM = 4, 64
C_WIDTH = C_HEADS * C_DIM

NP = 4096
COL_A = 0
COL_BZ = 1024
COL_BQKV = 1536
COL_C = 3072
COL_BG = 3840

LANE = 128
SUB = 8
VMEM_LIMIT = 56 * 1024 * 1024

_NN = (((1,), (0,)), ((), ()))
_NT = (((1,), (1,)), ((), ()))
_TN = (((0,), (0,)), ((), ()))


def _dot(a, b):
    return lax.dot_general(a, b, _NN, preferred_element_type=F32)


def _dot_nt(a, b):
    return lax.dot_general(a, b, _NT, preferred_element_type=F32)


def _dot_tn(a, b):
    return lax.dot_general(a, b, _TN, preferred_element_type=F32)


def _iota2(shape, dim):
    return lax.broadcasted_iota(jnp.int32, shape, dim)


def _sigmoid(x):
    return jax.nn.sigmoid(x)


def _softplus(x):
    return jnp.maximum(x, 0.0) + jnp.log1p(jnp.exp(-jnp.abs(x)))


def _split2(x):
    hi = x.astype(BF16)
    return hi, (x - hi.astype(F32)).astype(BF16)


def _split3(x):
    hi = x.astype(BF16)
    r1 = x - hi.astype(F32)
    mid = r1.astype(BF16)
    lo = (r1 - mid.astype(F32)).astype(BF16)
    return hi, mid, lo


def _mm(a, b, dims=_NN):
    ah, al = a if isinstance(a, tuple) else _split2(a)
    bh, bl = b if isinstance(b, tuple) else _split2(b)
    dg = lambda x, y: lax.dot_general(x, y, dims, preferred_element_type=F32)
    return dg(ah, bh) + dg(ah, bl) + dg(al, bh)


def _mm1(a, b, dims=_NN):
    return lax.dot_general(a.astype(BF16), b.astype(BF16), dims, preferred_element_type=F32)


def _mm01(x, e, dims=_NN, e_left=False):
    if e_left:
        dg = lambda p: lax.dot_general(e, p, dims, preferred_element_type=F32)
    else:
        dg = lambda p: lax.dot_general(p, e, dims, preferred_element_type=F32)
    hi, mid, lo = _split3(x)
    return dg(hi) + dg(mid) + dg(lo)


def _params(sem):
    return pltpu.CompilerParams(dimension_semantics=sem, vmem_limit_bytes=VMEM_LIMIT)


def _ada_kernel(c_ref, w_ref, b_ref, o_ref):
    c = c_ref[...]
    s = c * _sigmoid(c)
    o_ref[0] = _dot(s.astype(BF16), w_ref[0].astype(BF16)) + b_ref[0]


def _ada(c_all, w_ada, b_ada):
    depth, d, n = w_ada.shape
    r = c_all.shape[0]
    tn = 1024
    return pl.pallas_call(
        _ada_kernel,
        grid=(depth, n // tn),
        in_specs=[pl.BlockSpec((r, d), lambda l, j: (0, 0)),
                  pl.BlockSpec((1, d, tn), lambda l, j: (l, 0, j)),
                  pl.BlockSpec((1, 1, tn), lambda l, j: (l, 0, j))],
        out_specs=pl.BlockSpec((1, r, tn), lambda l, j: (l, 0, j)),
        out_shape=jax.ShapeDtypeStruct((depth, r, n), F32),
        compiler_params=_params(("parallel", "parallel")),
        name="ada",
    )(c_all, w_ada, b_ada.reshape(depth, 1, n))


def _rms_mod(x, g, sc, sh):
    ms = jnp.mean(x * x, axis=-1, keepdims=True)
    h = x * lax.rsqrt(ms + NORM_EPS) * g
    return h * (1.0 + sc) + sh


def _inproj_kernel(x_ref, g_ref, sc_ref, sh_ref, w_ref, o_ref, *, nchunk):
    h = _rms_mod(x_ref[0], g_ref[...], sc_ref[0], sh_ref[0]).astype(BF16)
    for n in range(NP // nchunk):
        o_ref[0, :, n * nchunk:(n + 1) * nchunk] = _dot(h, w_ref[:, n * nchunk:(n + 1) * nchunk])


def _mod_spec(mod, tm, d, ngrid):
    per_row = mod.shape[1] != 1
    if ngrid == 2:
        imap = (lambda b, i: (b, i, 0)) if per_row else (lambda b, i: (b, 0, 0))
    else:
        imap = (lambda b, i, f: (b, i, 0)) if per_row else (lambda b, i, f: (b, 0, 0))
    return pl.BlockSpec((1, tm if per_row else 1, d), imap)


def _inproj(x, g, sc, sh, w, tm):
    bn, t, d = x.shape
    return pl.pallas_call(
        functools.partial(_inproj_kernel, nchunk=1024),
        grid=(bn, t // tm),
        in_specs=[pl.BlockSpec((1, tm, d), lambda b, i: (b, i, 0)),
                  pl.BlockSpec((1, d), lambda b, i: (0, 0)),
                  _mod_spec(sc, tm, d, 2),
                  _mod_spec(sh, tm, d, 2),
                  pl.BlockSpec((d, NP), lambda b, i: (0, 0))],
        out_specs=pl.BlockSpec((1, tm, NP), lambda b, i: (b, i, 0)),
        out_shape=jax.ShapeDtypeStruct((bn, t, NP), F32),
        compiler_params=_params(("parallel", "parallel")),
        name="inproj",
    )(x, g.reshape(1, d), sc, sh, w)


def _tail_kernel(x_ref, ya_ref, yb_ref, yc_ref, gt1_ref, sh2_ref, sc2_ref, gt2_ref, g2_ref, gf_ref,
                 wo_ref, wg_ref, wu_ref, wd_ref, o_ref, x1_s, h_s, acc_s, *, nf, final_norm):
    f = pl.program_id(2)

    @pl.when(f == 0)
    def _():
        mix = (_dot(ya_ref[0].astype(BF16), wo_ref[0:A_WIDTH])
               + _dot(yb_ref[0].astype(BF16), wo_ref[A_WIDTH:A_WIDTH + B_WIDTH])
               + _dot(yc_ref[0].astype(BF16), wo_ref[A_WIDTH + B_WIDTH:A_WIDTH + B_WIDTH + C_WIDTH]))
        x1 = x_ref[0] + gt1_ref[0] * mix
        x1_s[...] = x1
        h_s[...] = _rms_mod(x1, g2_ref[...], sc2_ref[0], sh2_ref[0]).astype(BF16)
        acc_s[...] = jnp.zeros_like(acc_s)

    hb = h_s[...]
    gate = _dot(hb, wg_ref[...])
    up = _dot(hb, wu_ref[...])
    act = (gate * _sigmoid(gate)) * up
    acc_s[...] += _dot(act.astype(BF16), wd_ref[...])

    @pl.when(f == nf - 1)
    def _():
        y = x1_s[...] + gt2_ref[0] * acc_s[...]
        if final_norm:
            ms = jnp.mean(y * y, axis=-1, keepdims=True)
            y = y * lax.rsqrt(ms + NORM_EPS) * gf_ref[...]
        o_ref[0] = y


def _tail(x, ya, yb, yc, gt1, sh2, sc2, gt2, g2, gf, wo, wgu, wd, tm, tf, final_norm):
    bn, t, d = x.shape
    dff = wd.shape[0]
    nf = dff // tf
    row = lambda w: pl.BlockSpec((1, tm, w), lambda b, i, f: (b, i, 0))
    vec = pl.BlockSpec((1, d), lambda b, i, f: (0, 0))
    return pl.pallas_call(
        functools.partial(_tail_kernel, nf=nf, final_norm=final_norm),
        grid=(bn, t // tm, nf),
        in_specs=[row(d), row(A_WIDTH), row(B_WIDTH), row(C_WIDTH),
                  _mod_spec(gt1, tm, d, 3), _mod_spec(sh2, tm, d, 3), _mod_spec(sc2, tm, d, 3),
                  _mod_spec(gt2, tm, d, 3), vec, vec,
                  pl.BlockSpec((d, d), lambda b, i, f: (0, 0)),
                  pl.BlockSpec((d, tf), lambda b, i, f: (0, f)),
                  pl.BlockSpec((d, tf), lambda b, i, f: (0, f + nf)),
                  pl.BlockSpec((tf, d), lambda b, i, f: (f, 0))],
        out_specs=pl.BlockSpec((1, tm, d), lambda b, i, f: (b, i, 0)),
        out_shape=jax.ShapeDtypeStruct((bn, t, d), F32),
        scratch_shapes=[pltpu.VMEM((tm, d), F32), pltpu.VMEM((tm, d), BF16), pltpu.VMEM((tm, d), F32)],
        compiler_params=_params(("parallel", "parallel", "arbitrary")),
        name="tail",
    )(x, ya, yb, yc, gt1, sh2, sc2, gt2, g2.reshape(1, d), gf.reshape(1, d), wo, wgu, wgu, wd)


def _unit_lower_inverses(mats, nil, mm, pre):
    n = mats[0].shape[0]
    eye = (_iota2((n, n), 0) == _iota2((n, n), 1)).astype(F32)
    p = [-a for a in mats]
    t = [eye + m for m in p]
    k = 2
    while k < nil:
        ps = [pre(x) for x in p]
        p = [mm(x, x) for x in ps]
        t = [x + mm(x, y) for x, y in zip(t, p)]
        k *= 2
    return t


def _tri_masks(c):
    rows = _iota2((c, c), 0)
    cols = _iota2((c, c), 1)
    return rows >= cols, rows > cols, rows <= cols


def _block_tri_masks(nh, c):
    n = nh * c
    rows = _iota2((n, n), 0)
    cols = _iota2((n, n), 1)
    same = (rows // c) == (cols // c)
    return jnp.logical_and(same, rows >= cols), jnp.logical_and(same, rows > cols)


def _head_blocks(x, nh):
    t, n = x.shape
    w = n // nh
    tiled = jnp.concatenate([x] * nh, axis=0)
    keep = (_iota2((nh * t, n), 0) // t) == (_iota2((nh * t, n), 1) // w)
    return jnp.where(keep, tiled, jnp.zeros_like(tiled))


def _sum_row_blocks(x, nh):
    t = x.shape[0] // nh
    out = x[0:t]
    for h in range(1, nh):
        out = out + x[h * t:(h + 1) * t]
    return out


def _scan_chunks(lg_ref, bh_ref, st_s, y_s, c, ncb, heads, dv):
    s = [st_s[h] for h in range(heads)]
    for ck in range(ncb):
        for h in range(heads):
            z = _mm(lg_ref[0, ck, h], s[h]) + bh_ref[0, ck, h]
            y_s[ck * c:(ck + 1) * c, h * dv:(h + 1) * dv] = z[0:c]
            s[h] = z[c:]
    for h in range(heads):
        st_s[h] = s[h]


def _scan_specs(heads, c, dk, dv, ncb):
    return [pl.BlockSpec((1, ncb, heads, c + dk, dk), lambda b, i: (b, i, 0, 0, 0)),
            pl.BlockSpec((1, ncb, heads, c + dk, dv), lambda b, i: (b, i, 0, 0, 0)),
            pl.BlockSpec((1, heads, dk, dv), lambda b, i: (b, 0, 0, 0))]


def _prev_rows(proj, t, rb, width, colblk):
    bn = proj.shape[0]
    if t == rb:
        return jnp.zeros((bn, SUB, width), F32), pl.BlockSpec((1, SUB, width), lambda b, i: (b, 0, 0))
    return proj, pl.BlockSpec((1, SUB, width),
                              lambda b, i: (b, jnp.maximum(i * (rb // SUB) - 1, 0), colblk))


def _rwkv_a_kernel(p_ref, pp_ref, sh0_ref, mu_ref, w0_ref, w2_ref, a0_ref, a2_ref, g2_ref, kk_ref, ka_ref,
                   rk_ref, lg_ref, bh_ref, eg_ref, ext_s, *, tb, c, nck):
    i = pl.program_id(1)
    nw = A_WIDTH
    rc = nck * c
    rb = nck * tb

    prev = jnp.where(i == 0, sh0_ref[0], pp_ref[0, SUB - 1:SUB, :])
    ext_s[SUB - 1:SUB] = prev
    ext_s[SUB:SUB + rb] = p_ref[0]
    if tb < c:
        ext_s[SUB + rb:SUB + rc] = jnp.zeros((rc - rb, ext_s.shape[1]), F32)
    p = ext_s[SUB:SUB + rc]
    p_prev = ext_s[SUB - 1:SUB - 1 + rc]
    m = p + (p_prev - p) * mu_ref[...]

    r = m[:, 0:nw]
    k = m[:, nw:2 * nw]
    v = m[:, 2 * nw:3 * nw]
    pw = m[:, 3 * nw:3 * nw + 64]
    pa = m[:, 3 * nw + 64:3 * nw + 128]
    pg = m[:, 3 * nw + 128:3 * nw + 256]

    w = -_softplus(-(w0_ref[...] + _mm1(jnp.tanh(pw), w2_ref[...]))) - 0.5
    logd = -jnp.exp(w)
    a = _sigmoid(a0_ref[...] + _mm1(pa, a2_ref[...]))
    g = _mm1(_sigmoid(pg), g2_ref[...])

    ones_blk = ((_iota2((nw, nw), 0) // A_DIM) == (_iota2((nw, nw), 1) // A_DIM)).astype(BF16)
    hsum = lambda t: _mm01(t, ones_blk)

    kkf = k * kk_ref[...]
    kk = kkf / jnp.maximum(jnp.sqrt(hsum(kkf * kkf)), 1e-12)
    k2 = k * (1.0 + (a - 1.0) * ka_ref[...])

    if tb < c:
        valid = _iota2((rc, nw), 0) < tb
        zero = jnp.zeros((rc, nw), F32)
        logd = jnp.where(valid, logd, zero)
        kk = jnp.where(valid, kk, zero)
        k2 = jnp.where(valid, k2, zero)
        v = jnp.where(valid, v, zero)

    bonus = hsum(r * k2 * rk_ref[...]) * v
    eg_ref[0, :, 0:nw] = bonus[0:rb]
    eg_ref[0, :, nw:2 * nw] = g[0:rb]

    lincl = _tri_masks(c)[0].astype(BF16)
    nh = A_HEADS
    hc = nh * c
    incl, strict = _block_tri_masks(nh, c)
    zb = jnp.zeros((hc, hc), F32)
    eye = (_iota2((nw, nw), 0) == _iota2((nw, nw), 1)).astype(F32)
    each = lambda f, *lists: [f(*xs) for xs in zip(*lists)]
    rs = [slice(ck * c, (ck + 1) * c) for ck in range(nck)]
    logd_c = [logd[s] for s in rs]
    cum = each(lambda x: _mm01(x, lincl, e_left=True), logd_c)
    e_in = each(jnp.exp, cum)
    e_ex = each(lambda x, y: jnp.exp(x - y), cum, logd_c)
    e_neg = each(lambda x: jnp.exp(-x), cum)
    pc = [x[c - 1:c] for x in e_in]
    kkt = each(lambda s, e: _head_blocks(kk[s] * e, nh), rs, e_ex)
    rt = each(lambda s, e: _head_blocks(r[s] * e, nh), rs, e_in)
    kah = each(lambda s, e: _head_blocks(kk[s] * a[s] * e, nh), rs, e_neg)
    kh = each(lambda s, e: _head_blocks(k2[s] * e, nh), rs, e_neg)
    mm = _mm1
    b16 = lambda x: x.astype(BF16)
    v_s = each(lambda s: b16(_head_blocks(v[s], nh)), rs)
    pr = each(lambda x1, x2, y1, y2: mm(jnp.concatenate([x1, x2], axis=0), jnp.concatenate([y1, y2], axis=0),
                                        _NT), kkt, rt, kah, kh)
    la = [jnp.where(strict, x[0:hc, 0:hc], zb) for x in pr]
    lk = [jnp.where(strict, x[0:hc, hc:2 * hc], zb) for x in pr]
    ma = [jnp.where(incl, x[hc:2 * hc, 0:hc], zb) for x in pr]
    mk = [jnp.where(incl, x[hc:2 * hc, hc:2 * hc], zb) for x in pr]
    tinv = _unit_lower_inverses(la, c, mm, b16)
    lkv = each(mm, lk, v_s)
    w1b1 = each(lambda t, x, y: b16(-mm(t, jnp.concatenate([x, y], axis=1))), tinv, kkt, lkv)
    mw = each(mm, ma, w1b1)
    mkv = each(mm, mk, v_s)
    gh = each(lambda x, p, y: mm(x * p, y, _TN), kah, pc, w1b1)
    khv = each(lambda x, p, y: mm(x * p, y, _TN), kh, pc, v_s)
    for ck in range(nck):
        lg_ref[0, ck, 0:c] = _sum_row_blocks(rt[ck] + mw[ck][:, 0:nw], nh)
        lg_ref[0, ck, c:c + A_DIM] = _sum_row_blocks(eye * pc[ck] + gh[ck][:, 0:nw], nh)
        bh_ref[0, ck, 0:c] = _sum_row_blocks(mw[ck][:, nw:2 * nw] + mkv[ck], nh)
        bh_ref[0, ck, c:c + A_DIM] = _sum_row_blocks(gh[ck][:, nw:2 * nw] + khv[ck], nh)


def _rwkv_b_kernel(lg_ref, bh_ref, s0_ref, eg_ref, lw_ref, lb_ref, y_ref, sn_ref, st_s, y_s, *, tb, c, ncb):
    nw = A_WIDTH
    nh = A_HEADS

    @pl.when(pl.program_id(1) == 0)
    def _():
        st_s[...] = _head_blocks(s0_ref[0], nh)

    s = st_s[...]
    rbk = c + A_DIM
    for ck in range(ncb):
        z = _mm(_head_blocks(lg_ref[0, ck], nh), s)
        y_c = bh_ref[0, ck, 0:c]
        for h in range(nh):
            y_c = y_c + z[h * rbk:h * rbk + c]
        y_s[ck * c:(ck + 1) * c] = y_c
        s = (jnp.concatenate([z[h * rbk + c:(h + 1) * rbk] for h in range(nh)], axis=0)
             + _head_blocks(bh_ref[0, ck, c:rbk], nh))
    st_s[...] = s
    rb = ncb * tb
    y = y_s[...]
    ones_blk = ((_iota2((nw, nw), 0) // A_DIM) == (_iota2((nw, nw), 1) // A_DIM)).astype(BF16)
    hsum = lambda t: _mm01(t, ones_blk)
    mean = hsum(y) * (1.0 / A_DIM)
    yc = y - mean
    var = hsum(yc * yc) * (1.0 / A_DIM)
    yn = yc * lax.rsqrt(var + A_LNX_EPS) * lw_ref[...] + lb_ref[...]
    eg = eg_ref[0]
    y_ref[0] = (yn[0:rb] + eg[:, 0:nw]) * eg[:, nw:2 * nw]
    sn_ref[0] = _sum_row_blocks(s, nh)


def _rwkv(proj, shift0, s0, lp, c, nck, ncb):
    bn, t, _ = proj.shape
    tb = min(t, c)
    nck = nck if tb == c else 1
    ncb = ncb if tb == c else 1
    nc = t // tb
    cols = 4 * A_WIDTH
    rb = nck * tb
    vec = lambda n: pl.BlockSpec((1, n), lambda b, i: (0, 0))
    mat = lambda r, n: pl.BlockSpec((r, n), lambda b, i: (0, 0))
    row = lambda a: a.reshape(1, -1)
    pp, pp_spec = _prev_rows(proj, t, rb, cols, COL_A // cols)
    lg_shape = jax.ShapeDtypeStruct((bn, nc, c + A_DIM, A_WIDTH), F32)
    lg_spec = pl.BlockSpec((1, nck, c + A_DIM, A_WIDTH), lambda b, i: (b, i, 0, 0))
    lg, bh, eg = pl.pallas_call(
        functools.partial(_rwkv_a_kernel, tb=tb, c=c, nck=nck),
        grid=(bn, nc // nck),
        in_specs=[pl.BlockSpec((1, rb, cols), lambda b, i: (b, i, COL_A // cols)),
                  pp_spec,
                  pl.BlockSpec((1, 1, cols), lambda b, i: (b, 0, 0)),
                  vec(cols), vec(A_WIDTH), mat(64, A_WIDTH), vec(A_WIDTH), mat(64, A_WIDTH), mat(128, A_WIDTH),
                  vec(A_WIDTH), vec(A_WIDTH), vec(A_WIDTH)],
        out_specs=[lg_spec, lg_spec, pl.BlockSpec((1, rb, 2 * A_WIDTH), lambda b, i: (b, i, 0))],
        out_shape=[lg_shape, lg_shape, jax.ShapeDtypeStruct((bn, t, 2 * A_WIDTH), F32)],
        scratch_shapes=[pltpu.VMEM((SUB + nck * c, cols), F32)],
        compiler_params=_params(("parallel", "parallel")),
        name="rwkv_a",
    )(proj, pp, shift0.reshape(bn, 1, cols), row(lp["a_mu"]), row(lp["a_w0"]), lp["a_w2"], row(lp["a_a0"]),
      lp["a_a2"], lp["a_g2"], row(lp["a_k_k"]), row(lp["a_k_a"]), row(lp["a_r_k"]))
    y, st = pl.pallas_call(
        functools.partial(_rwkv_b_kernel, tb=tb, c=c, ncb=ncb),
        grid=(bn, nc // ncb),
        in_specs=[pl.BlockSpec((1, ncb, c + A_DIM, A_WIDTH), lambda b, i: (b, i, 0, 0)),
                  pl.BlockSpec((1, ncb, c + A_DIM, A_WIDTH), lambda b, i: (b, i, 0, 0)),
                  pl.BlockSpec((1, A_DIM, A_WIDTH), lambda b, i: (b, 0, 0)),
                  pl.BlockSpec((1, ncb * tb, 2 * A_WIDTH), lambda b, i: (b, i, 0)), vec(A_WIDTH), vec(A_WIDTH)],
        out_specs=[pl.BlockSpec((1, ncb * tb, A_WIDTH), lambda b, i: (b, i, 0)),
                   pl.BlockSpec((1, A_DIM, A_WIDTH), lambda b, i: (b, 0, 0))],
        out_shape=[jax.ShapeDtypeStruct((bn, t, A_WIDTH), F32),
                   jax.ShapeDtypeStruct((bn, A_DIM, A_WIDTH), F32)],
        scratch_shapes=[pltpu.VMEM((A_WIDTH, A_WIDTH), F32), pltpu.VMEM((ncb * c, A_WIDTH), F32)],
        compiler_params=_params(("parallel", "arbitrary")),
        name="rwkv_b",
    )(lg, bh, jnp.transpose(s0, (0, 3, 1, 2)).reshape(bn, A_DIM, A_WIDTH), eg,
      row(lp["a_lnx_w"]), row(lp["a_lnx_b"]))
    return y, jnp.transpose(st.reshape(bn, A_DIM, A_HEADS, A_DIM), (0, 2, 3, 1))


def _gdn_a_kernel(qkv_ref, qp_ref, bg_ref, cv0_ref, cw_ref, alog_ref, dtb_ref, lg_ref, bh_ref, ext_s, bg_s,
                  *, tb, c, nck):
    i = pl.program_id(1)
    rc = nck * c
    rb = nck * tb

    ext_s[0:SUB] = jnp.where(i == 0, cv0_ref[0], qp_ref[0])
    ext_s[SUB:SUB + rb] = qkv_ref[0]
    if tb < c:
        ext_s[SUB + rb:SUB + rc] = jnp.zeros((rc - rb, ext_s.shape[1]), F32)
    cw = cw_ref[...]
    u = ext_s[SUB - 3:SUB - 3 + rc] * cw[0:1]
    for j in range(1, B_CONV):
        u = u + ext_s[SUB - 3 + j:SUB - 3 + j + rc] * cw[j:j + 1]
    qkv = u * _sigmoid(u)

    nb = bg_s.shape[1]
    bg_s[0:rb] = bg_ref[0]
    if tb < c:
        bg_s[rb:rc] = jnp.zeros((rc - rb, nb), F32)
    bg = bg_s[...]
    beta_all = _sigmoid(bg)
    gl = -jnp.exp(alog_ref[...]) * _softplus(bg + dtb_ref[...])
    if tb < c:
        valid = _iota2((rc, nb), 0) < tb
        beta_all = jnp.where(valid, beta_all, jnp.zeros_like(beta_all))
        gl = jnp.where(valid, gl, jnp.zeros_like(gl))
    gl = gl[:, 0:LANE]

    lincl, _, upper = _tri_masks(c)
    nh = B_HEADS
    hc = nh * c
    incl, strict = _block_tri_masks(nh, c)
    zb = jnp.zeros((hc, hc), F32)
    own_head = (_iota2((hc, B_WIDTH), 0) // c) == (_iota2((hc, B_WIDTH), 1) // B_DIM)
    eye_rows = (_iota2((B_WIDTH, B_DIM), 0) % B_DIM) == _iota2((B_WIDTH, B_DIM), 1)
    each = lambda f, *lists: [f(*xs) for xs in zip(*lists)]
    rs = [slice(ck * c, (ck + 1) * c) for ck in range(nck)]
    heads_on_rows = lambda s, col0: jnp.concatenate(
        [qkv[s, col0 + h * B_DIM:col0 + (h + 1) * B_DIM] for h in range(nh)], axis=0)
    l2n = lambda x: x * lax.rsqrt(jnp.sum(x * x, axis=-1, keepdims=True) + L2_EPS)
    gc = each(lambda s: _mm01(gl[s], lincl.astype(BF16), e_left=True), rs)
    gct = each(lambda s: _mm01(gl[s], upper.astype(BF16), _TN), rs)
    q_r = each(lambda s: l2n(heads_on_rows(s, 0)) * (B_DIM ** -0.5), rs)
    k_r = each(lambda s: l2n(heads_on_rows(s, B_WIDTH)), rs)
    v_r = each(lambda s: heads_on_rows(s, 2 * B_WIDTH), rs)
    gcol = [jnp.concatenate([x[:, nh + h:nh + h + 1] for h in range(nh)], axis=0) for x in gc]
    grow = [jnp.concatenate([x[nh + h:nh + h + 1, :] for h in range(nh)], axis=1) for x in gct]
    bcol = [jnp.concatenate([beta_all[s, h:h + 1] for h in range(nh)], axis=0) for s in rs]
    glast = [jnp.concatenate([jnp.broadcast_to(x[c - 1:c, nh + h:nh + h + 1], (c, 1)) for h in range(nh)],
                             axis=0) for x in gc]
    eglast = [jnp.concatenate([jnp.broadcast_to(jnp.exp(x[c - 1:c, nh + h:nh + h + 1]), (B_DIM, 1))
                               for h in range(nh)], axis=0) for x in gc]
    decay = each(lambda x, y: jnp.where(incl, jnp.exp(jnp.where(incl, x - y, zb)), zb), gcol, grow)
    kb = each(lambda x, y: x * y, k_r, bcol)
    k_s = each(_split2, k_r)
    amat = each(lambda x, y, d: jnp.where(strict, _mm(x, y, _NT) * d, zb), kb, k_s, decay)
    qk = each(lambda x, y, d: _mm(x, y, _NT) * d, q_r, k_s, decay)
    tinv = _unit_lower_inverses(amat, c, _mm, _split2)
    eg = each(jnp.exp, gcol)
    uu = each(lambda t, x, b, y, e: _split2(_mm(t, jnp.concatenate([x * b, y * e], axis=1))),
              tinv, v_r, bcol, kb, eg)
    qu = each(_mm, qk, uu)
    kd_blk = each(lambda x, gl_, gc_: jnp.where(own_head, jnp.concatenate([x * jnp.exp(gl_ - gc_)] * nh, axis=1),
                                                jnp.zeros((hc, B_WIDTH), F32)), k_r, glast, gcol)
    ku = each(lambda x, y: _mm(x, y, _TN), kd_blk, uu)
    for ck in range(nck):
        w2 = q_r[ck] * eg[ck] - qu[ck][:, B_DIM:]
        g = jnp.where(eye_rows, eglast[ck], jnp.zeros((B_WIDTH, B_DIM), F32)) - ku[ck][:, B_DIM:]
        for h in range(nh):
            lg_ref[0, ck, h, 0:c] = w2[h * c:(h + 1) * c]
            lg_ref[0, ck, h, c:c + B_DIM] = g[h * B_DIM:(h + 1) * B_DIM]
            bh_ref[0, ck, h, 0:c] = qu[ck][h * c:(h + 1) * c, 0:B_DIM]
            bh_ref[0, ck, h, c:c + B_DIM] = ku[ck][h * B_DIM:(h + 1) * B_DIM, 0:B_DIM]


def _gdn_b_kernel(lg_ref, bh_ref, s0_ref, z_ref, nw_ref, y_ref, sn_ref, st_s, y_s, *, tb, c, ncb):
    @pl.when(pl.program_id(1) == 0)
    def _():
        st_s[...] = s0_ref[0]

    _scan_chunks(lg_ref, bh_ref, st_s, y_s, c, ncb, B_HEADS, B_DIM)
    rb = ncb * tb
    z = z_ref[0]
    for h in range(B_HEADS):
        sl = slice(h * B_DIM, (h + 1) * B_DIM)
        o = y_s[:, sl]
        on = o * lax.rsqrt(jnp.mean(o * o, axis=-1, keepdims=True) + NORM_EPS) * nw_ref[...]
        z_h = z[:, sl]
        y_ref[0, :, sl] = on[0:rb] * (z_h * _sigmoid(z_h))
    sn_ref[0] = st_s[...]


def _gdn(proj, conv0, s0, lp, c, nck, ncb):
    bn, t, _ = proj.shape
    tb = min(t, c)
    nck = nck if tb == c else 1
    ncb = ncb if tb == c else 1
    nc = t // tb
    rb = nck * tb
    nb = NP - COL_BG
    cv0 = jnp.concatenate([jnp.zeros((bn, SUB - (B_CONV - 1), B_QKV), F32), conv0], axis=1)
    lane8 = lambda a: jnp.zeros((1, nb), F32).at[0, B_HEADS:2 * B_HEADS].set(a)
    qp, qp_spec = _prev_rows(proj, t, rb, B_QKV, COL_BQKV // B_QKV)
    lg_shape = jax.ShapeDtypeStruct((bn, nc, B_HEADS, c + B_DIM, B_DIM), F32)
    lg_spec = pl.BlockSpec((1, nck, B_HEADS, c + B_DIM, B_DIM), lambda b, i: (b, i, 0, 0, 0))
    lg, bh = pl.pallas_call(
        functools.partial(_gdn_a_kernel, tb=tb, c=c, nck=nck),
        grid=(bn, nc // nck),
        in_specs=[pl.BlockSpec((1, rb, B_QKV), lambda b, i: (b, i, COL_BQKV // B_QKV)),
                  qp_spec,
                  pl.BlockSpec((1, rb, nb), lambda b, i: (b, i, COL_BG // nb)),
                  pl.BlockSpec((1, SUB, B_QKV), lambda b, i: (b, 0, 0)),
                  pl.BlockSpec((B_CONV, B_QKV), lambda b, i: (0, 0)),
                  pl.BlockSpec((1, nb), lambda b, i: (0, 0)),
                  pl.BlockSpec((1, nb), lambda b, i: (0, 0))],
        out_specs=[lg_spec, lg_spec],
        out_shape=[lg_shape, lg_shape],
        scratch_shapes=[pltpu.VMEM((SUB + nck * c, B_QKV), F32), pltpu.VMEM((nck * c, nb), F32)],
        compiler_params=_params(("parallel", "parallel")),
        name="gdn_a",
    )(proj, qp, proj, cv0, lp["b_conv_w"], lane8(lp["b_a_log"]), lane8(lp["b_dt_bias"]))
    return pl.pallas_call(
        functools.partial(_gdn_b_kernel, tb=tb, c=c, ncb=ncb),
        grid=(bn, nc // ncb),
        in_specs=_scan_specs(B_HEADS, c, B_DIM, B_DIM, ncb)
                 + [pl.BlockSpec((1, ncb * tb, B_WIDTH), lambda b, i: (b, i, COL_BZ // B_WIDTH)),
                    pl.BlockSpec((1, B_DIM), lambda b, i: (0, 0))],
        out_specs=[pl.BlockSpec((1, ncb * tb, B_WIDTH), lambda b, i: (b, i, 0)),
                   pl.BlockSpec((1, B_HEADS, B_DIM, B_DIM), lambda b, i: (b, 0, 0, 0))],
        out_shape=[jax.ShapeDtypeStruct((bn, t, B_WIDTH), F32),
                   jax.ShapeDtypeStruct((bn, B_HEADS, B_DIM, B_DIM), F32)],
        scratch_shapes=[pltpu.VMEM((B_HEADS, B_DIM, B_DIM), F32), pltpu.VMEM((ncb * c, B_WIDTH), F32)],
        compiler_params=_params(("parallel", "arbitrary")),
        name="gdn_b",
    )(lg, bh, s0, proj, lp["b_norm_w"].reshape(1, B_DIM))


def _sbp_kernel(bias_ref, q_ref, k_ref, v_ref, o_ref, carry_s, acc_s, *, tq, tk):
    qi = pl.program_id(1)
    j = pl.program_id(2)
    nsub = tq // tk

    @pl.when(j == 0)
    def _():
        carry_s[...] = jnp.zeros_like(carry_s)
        acc_s[...] = jnp.zeros_like(acc_s)

    def sweep(diag):
        qb = (q_ref[0] * (C_DIM ** -0.5 * LOG2E)).astype(BF16)
        kb = k_ref[0].astype(BF16)
        vb = v_ref[0].astype(BF16)
        tri = (_iota2((tk, 2 * tk), 1) % tk >= _iota2((tk, 2 * tk), 0)).astype(BF16)
        hsl = lambda h: slice(h * C_DIM, (h + 1) * C_DIM)
        tiles = [(s, h) for s in reversed(range(nsub)) for h in range(C_HEADS)]
        valid = {s: (s * tk + _iota2((tk, tq), 0)) < _iota2((tk, tq), 1) if diag else None for s in range(nsub)}
        zt = {(s, h): _dot_nt(kb[s * tk:(s + 1) * tk, hsl(h)], qb[:, hsl(h)]) + bias_ref[h] * LOG2E
              for s, h in tiles}
        rc = {}
        for s, h in tiles:
            z = zt[s, h]
            sp = jnp.maximum(z, 0.0) + jnp.log2(1.0 + jnp.exp2(-jnp.abs(z)))
            if diag:
                sp = jnp.where(valid[s], sp, jnp.zeros_like(sp))
            hi, lo = _split2(sp)
            rc[s, h] = _dot(tri, jnp.concatenate([hi, lo], axis=0))
        carry = [carry_s[h:h + 1, :] for h in range(C_HEADS)]
        acc = [acc_s[h] for h in range(C_HEADS)]
        for s, h in tiles:
            at = jnp.exp2(zt[s, h] - rc[s, h] - carry[h])
            if diag:
                at = jnp.where(valid[s], at, jnp.zeros_like(at))
            acc[h] = acc[h] + _dot_tn(vb[s * tk:(s + 1) * tk, hsl(h)], at.astype(BF16))
            carry[h] = carry[h] + rc[s, h][0:1]
        for h in range(C_HEADS):
            carry_s[h:h + 1, :] = carry[h]
            acc_s[h] = acc[h]

    @pl.when(j == 0)
    def _():
        sweep(True)

    @pl.when(jnp.logical_and(j > 0, j <= qi))
    def _():
        sweep(False)

    @pl.when(j == qi)
    def _():
        o_ref[0] = acc_s[...].reshape(C_WIDTH, tq).T


def _sb_prompt(proj, bias, tq, tk):
    bn, t, _ = proj.shape
    nq = t // tq
    qcol = COL_C // C_WIDTH
    return pl.pallas_call(
        functools.partial(_sbp_kernel, tq=tq, tk=tk),
        grid_spec=pltpu.PrefetchScalarGridSpec(
            num_scalar_prefetch=0,
            grid=(bn, nq, nq),
            in_specs=[pl.BlockSpec(memory_space=pltpu.SMEM),
                      pl.BlockSpec((1, tq, C_WIDTH), lambda b, i, j: (b, i, qcol)),
                      pl.BlockSpec((1, tq, C_WIDTH), lambda b, i, j: (b, jnp.maximum(i - j, 0), qcol + 1)),
                      pl.BlockSpec((1, tq, C_WIDTH), lambda b, i, j: (b, jnp.maximum(i - j, 0), qcol + 2))],
            out_specs=pl.BlockSpec((1, tq, C_WIDTH), lambda b, i, j: (b, i, 0)),
            scratch_shapes=[pltpu.VMEM((C_HEADS, tq), F32), pltpu.VMEM((C_HEADS, C_DIM, tq), F32)]),
        out_shape=jax.ShapeDtypeStruct((bn, t, C_WIDTH), F32),
        compiler_params=_params(("parallel", "parallel", "arbitrary")),
        name="sbp",
    )(bias, proj, proj, proj)


def _sbs_kernel(pt_ref, bias_ref, q_ref, k_ref, v_ref, *rest, tn, npage, ps):
    kp_refs = rest[0:npage]
    vp_refs = rest[npage:2 * npage]
    o_ref, q_s, carry_s, acc_s, kown_s, vown_s = rest[2 * npage:]
    j = pl.program_id(1)
    nj = pl.num_programs(1)
    rows = C_HEADS * SUB
    tri = (_iota2((ps, ps), 0) >= _iota2((ps, ps), 1)).astype(BF16)
    bias_col = jnp.concatenate([jnp.full((SUB, 1), bias_ref[h], F32) for h in range(C_HEADS)], axis=0)
    hsl = lambda h: slice(h * C_DIM, (h + 1) * C_DIM)
    rsl = lambda h: slice(h * SUB, (h + 1) * SUB)

    @pl.when(j == 0)
    def _():
        q_s[...] = jnp.zeros_like(q_s)
        q_s[0:tn] = q_ref[0] * (C_DIM ** -0.5)
        kown_s[...] = jnp.zeros_like(kown_s)
        vown_s[...] = jnp.zeros_like(vown_s)
        kown_s[0:tn] = k_ref[0]
        vown_s[0:tn] = v_ref[0]
        qb = q_s[...].astype(BF16)
        kb = kown_s[...].astype(BF16)
        vb = vown_s[...].astype(BF16)
        z = jnp.concatenate([_dot_nt(qb[:, hsl(h)], kb[:, hsl(h)]) for h in range(C_HEADS)], axis=0) + bias_col
        valid = _iota2((rows, ps), 1) < (_iota2((rows, ps), 0) % SUB)
        zero = jnp.zeros((rows, ps), F32)
        rc = _mm01(jnp.where(valid, _softplus(z), zero), tri)
        a = jnp.where(valid, jnp.exp(z - rc), zero)
        for h in range(C_HEADS):
            acc_s[h] = _dot(a[rsl(h)].astype(BF16), vb[:, hsl(h)])
        carry_s[...] = rc[:, 0:1]

    qb = q_s[...].astype(BF16)
    kt = [kp_refs[i][0, 0].astype(BF16) for i in range(npage)]
    vt = [vp_refs[i][0, 0].astype(BF16) for i in range(npage)]
    z = jnp.concatenate(
        [_dot(qb[:, hsl(h)], jnp.concatenate([kt[i][h] for i in range(npage)], axis=1)) for h in range(C_HEADS)],
        axis=0) + bias_col
    sp = _softplus(z)
    sp_rows = jnp.concatenate([sp[:, i * ps:(i + 1) * ps] for i in range(npage)], axis=0)
    hi, lo = _split2(sp_rows)
    rc_rows = _dot(jnp.concatenate([hi, lo], axis=1), jnp.concatenate([tri, tri], axis=0))
    carry = carry_s[...]
    a_segs = []
    for i in range(npage):
        seg = slice(i * ps, (i + 1) * ps)
        rc = rc_rows[i * rows:(i + 1) * rows]
        a_segs.append(jnp.exp(z[:, seg] - rc - carry))
        carry = carry + rc[:, 0:1]
    carry_s[...] = carry
    a = jnp.concatenate(a_segs, axis=1)
    for h in range(C_HEADS):
        v_h = jnp.concatenate([vt[i][h] for i in range(npage)], axis=1)
        acc_s[h] += _dot_nt(a[rsl(h)].astype(BF16), v_h)

    @pl.when(j == nj - 1)
    def _():
        for h in range(C_HEADS):
            o_ref[0, :, hsl(h)] = acc_s[h][0:tn]


def _sb_sample(proj, cache_kt, cache_vt, page_table, bias, layer, npage):
    bn, tn, _ = proj.shape
    n_pages = page_table.shape[1]
    ps = cache_kt.shape[-1]
    qcol = COL_C // C_WIDTH
    page_spec = lambda i: pl.BlockSpec(
        (1, 1, C_HEADS, C_DIM, ps),
        lambda b, j, pt: (layer, pt[b, n_pages - 1 - (j * npage + i)], 0, 0, 0))
    own = lambda off: pl.BlockSpec((1, tn, C_WIDTH), lambda b, j, pt: (b, 0, qcol + off))
    return pl.pallas_call(
        functools.partial(_sbs_kernel, tn=tn, npage=npage, ps=ps),
        grid_spec=pltpu.PrefetchScalarGridSpec(
            num_scalar_prefetch=1,
            grid=(bn, n_pages // npage),
            in_specs=[pl.BlockSpec(memory_space=pltpu.SMEM), own(0), own(1), own(2)]
                     + [page_spec(i) for i in range(npage)] + [page_spec(i) for i in range(npage)],
            out_specs=pl.BlockSpec((1, tn, C_WIDTH), lambda b, j, pt: (b, 0, 0)),
            scratch_shapes=[pltpu.VMEM((SUB, C_WIDTH), F32),
                            pltpu.VMEM((C_HEADS * SUB, 1), F32),
                            pltpu.VMEM((C_HEADS, SUB, C_DIM), F32),
                            pltpu.VMEM((ps, C_WIDTH), F32),
                            pltpu.VMEM((ps, C_WIDTH), F32)]),
        out_shape=jax.ShapeDtypeStruct((bn, tn, C_WIDTH), F32),
        compiler_params=_params(("parallel", "arbitrary")),
        name="sbs",
    )(page_table, bias, proj, proj, proj, *([cache_kt] * npage), *([cache_vt] * npage))


def _reorder_w_in(w):
    a_cols = 4 * A_WIDTH
    b0 = a_cols
    bz0 = b0 + B_QKV
    bg0 = bz0 + B_WIDTH
    c0 = bg0 + 2 * B_HEADS
    c1 = c0 + 3 * C_WIDTH
    pad = jnp.zeros((w.shape[0], NP - (COL_BG + 2 * B_HEADS)), w.dtype)
    return jnp.concatenate([w[:, 0:a_cols], w[:, bz0:bg0], w[:, b0:bz0], w[:, c0:c1], w[:, bg0:c0], pad],
                           axis=1).astype(BF16)


def kernel(x_prompt, x_sample, cache_k, cache_v, state_shift, state_wkv, state_conv, state_gdn, page_table,
           c_prompt, c_sample, w_ada, b_ada, norm1, norm2, w_in, w_out, a_mu, a_w0, a_w2, a_a0, a_a2, a_g2,
           a_k_k, a_k_a, a_r_k, a_lnx_w, a_lnx_b, b_conv_w, b_a_log, b_dt_bias, b_norm_w, c_bias, w_gu, w_down,
           norm_f):
    depth = w_in.shape[0]
    nb, seq, d = x_prompt.shape
    nd, tn, _ = x_sample.shape
    n_ada = w_ada.shape[2] // d

    r_all = -(-(nb + nd) // SUB) * SUB
    c_all = jnp.concatenate([c_prompt, c_sample, jnp.zeros((r_all - nb - nd, d), F32)], axis=0)
    mod = _ada(c_all, w_ada, b_ada)

    cache_kt = jnp.transpose(cache_k, (0, 1, 3, 4, 2))
    cache_vt = jnp.transpose(cache_v, (0, 1, 3, 4, 2))

    xp = x_prompt
    xs = x_sample.reshape(1, nd * tn, d)
    outs_p = [[] for _ in range(6)]
    outs_s = [[] for _ in range(6)]
    for l in range(depth):
        lp = dict(a_mu=a_mu[l], a_w0=a_w0[l], a_w2=a_w2[l], a_a0=a_a0[l], a_a2=a_a2[l], a_g2=a_g2[l],
                  a_k_k=a_k_k[l], a_k_a=a_k_a[l], a_r_k=a_r_k[l], a_lnx_w=a_lnx_w[l], a_lnx_b=a_lnx_b[l],
                  b_conv_w=b_conv_w[l], b_a_log=b_a_log[l], b_dt_bias=b_dt_bias[l], b_norm_w=b_norm_w[l])
        w_in_l = _reorder_w_in(w_in[l])
        w_out_l = w_out[l].astype(BF16)
        w_gu_l = w_gu[l].astype(BF16)
        w_down_l = w_down[l].astype(BF16)
        last = l == depth - 1

        mods_p = [m.reshape(nb, 1, d) for m in jnp.split(mod[l, 0:nb], n_ada, axis=-1)]
        mods_s = [jnp.repeat(m, tn, axis=0).reshape(1, nd * tn, d)
                  for m in jnp.split(mod[l, nb:nb + nd], n_ada, axis=-1)]

        sh1, sc1, gt1, sh2, sc2, gt2 = mods_p
        proj = _inproj(xp, norm1[l], sc1, sh1, w_in_l, tm=256)
        ya, wkv_p = _rwkv(proj, jnp.zeros((nb, 4 * A_WIDTH), F32),
                          jnp.zeros((nb, A_HEADS, A_DIM, A_DIM), F32), lp, c=64, nck=4, ncb=8)
        yb, gdn_p = _gdn(proj, jnp.zeros((nb, B_CONV - 1, B_QKV), F32),
                         jnp.zeros((nb, B_HEADS, B_DIM, B_DIM), F32), lp, c=64, nck=4, ncb=8)
        yc = _sb_prompt(proj, c_bias[l], tq=512, tk=128)
        xp = _tail(xp, ya, yb, yc, gt1, sh2, sc2, gt2, norm2[l], norm_f, w_out_l, w_gu_l, w_down_l,
                   tm=512, tf=1408, final_norm=last)
        k_new = proj[:, :, COL_C + C_WIDTH:COL_C + 2 * C_WIDTH].reshape(nb, seq, C_HEADS, C_DIM)
        v_new = proj[:, :, COL_C + 2 * C_WIDTH:COL_C + 3 * C_WIDTH].reshape(nb, seq, C_HEADS, C_DIM)
        shift_new = proj[:, seq - 1, COL_A:COL_A + 4 * A_WIDTH]
        conv_new = proj[:, seq - (B_CONV - 1):, COL_BQKV:COL_BQKV + B_QKV]
        for lst, val in zip(outs_p, (k_new, v_new, shift_new, wkv_p, conv_new, gdn_p)):
            lst.append(val)

        sh1, sc1, gt1, sh2, sc2, gt2 = mods_s
        proj = _inproj(xs, norm1[l], sc1, sh1, w_in_l, tm=nd * tn).reshape(nd, tn, NP)
        ya, wkv_s = _rwkv(proj, state_shift[l], state_wkv[l], lp, c=SUB, nck=1, ncb=1)
        yb, gdn_s = _gdn(proj, state_conv[l], state_gdn[l], lp, c=SUB, nck=1, ncb=1)
        yc = _sb_sample(proj, cache_kt, cache_vt, page_table, c_bias[l], l, npage=32)
        flat = lambda a: a.reshape(1, nd * tn, a.shape[-1])
        xs = _tail(xs, flat(ya), flat(yb), flat(yc), gt1, sh2, sc2, gt2, norm2[l], norm_f, w_out_l, w_gu_l,
                   w_down_l, tm=nd * tn, tf=256, final_norm=last)
        k_new = proj[:, :, COL_C + C_WIDTH:COL_C + 2 * C_WIDTH].reshape(nd, tn, C_HEADS, C_DIM)
        v_new = proj[:, :, COL_C + 2 * C_WIDTH:COL_C + 3 * C_WIDTH].reshape(nd, tn, C_HEADS, C_DIM)
        shift_new = proj[:, tn - 1, COL_A:COL_A + 4 * A_WIDTH]
        conv_new = jnp.concatenate([state_conv[l], proj[:, :, COL_BQKV:COL_BQKV + B_QKV]],
                                   axis=1)[:, -(B_CONV - 1):]
        for lst, val in zip(outs_s, (k_new, v_new, shift_new, wkv_s, conv_new, gdn_s)):
            lst.append(val)

    y_prompt = xp
    y_sample = xs.reshape(nd, tn, d)
    p_out = tuple(jnp.stack(t) for t in outs_p)
    s_out = tuple(jnp.stack(t) for t in outs_s)
    return (y_prompt, y_sample) + p_out + s_out
```

```python
import functools

import jax
import jax.numpy as jnp
from jax import lax
from jax.experimental import pallas as pl
from jax.experimental.pallas import tpu as pltpu

F32 = jnp.float32
BF16 = jnp.bfloat16

LOG2E = 1.4426950408889634
NORM_EPS = 1e-6
L2_EPS = 1e-6
A_LNX_EPS = 64e-5

A_HEADS, A_DIM = 4, 64
A_WIDTH = A_HEADS * A_DIM
B_HEADS, B_DIM = 4, 128
B_WIDTH = B_HEADS * B_DIM
B_QKV = 3 * B_WIDTH
B_CONV = 4
C_HEADS, C_DIM = 4, 64
C_WIDTH = C_HEADS * C_DIM

NP = 4096
COL_A = 0
COL_BZ = 1024
COL_BQKV = 1536
COL_C = 3072
COL_BG = 3840

LANE = 128
SUB = 8
VMEM_LIMIT = 56 * 1024 * 1024

_NN = (((1,), (0,)), ((), ()))
_NT = (((1,), (1,)), ((), ()))
_TN = (((0,), (0,)), ((), ()))


def _dot(a, b):
    return lax.dot_general(a, b, _NN, preferred_element_type=F32)


def _dot_nt(a, b):
    return lax.dot_general(a, b, _NT, preferred_element_type=F32)


def _dot_tn(a, b):
    return lax.dot_general(a, b, _TN, preferred_element_type=F32)


def _iota2(shape, dim):
    return lax.broadcasted_iota(jnp.int32, shape, dim)


def _sigmoid(x):
    return jax.nn.sigmoid(x)


def _softplus(x):
    return jnp.maximum(x, 0.0) + jnp.log1p(jnp.exp(-jnp.abs(x)))


def _split2(x):
    hi = x.astype(BF16)
    return hi, (x - hi.astype(F32)).astype(BF16)


def _split3(x):
    hi = x.astype(BF16)
    r1 = x - hi.astype(F32)
    mid = r1.astype(BF16)
    lo = (r1 - mid.astype(F32)).astype(BF16)
    return hi, mid, lo


def _mm(a, b, dims=_NN):
    ah, al = a if isinstance(a, tuple) else _split2(a)
    bh, bl = b if isinstance(b, tuple) else _split2(b)
    dg = lambda x, y: lax.dot_general(x, y, dims, preferred_element_type=F32)
    return dg(ah, bh) + dg(ah, bl) + dg(al, bh)


def _mm1(a, b, dims=_NN):
    return lax.dot_general(a.astype(BF16), b.astype(BF16), dims, preferred_element_type=F32)


def _mm01(x, e, dims=_NN, e_left=False):
    if e_left:
        dg = lambda p: lax.dot_general(e, p, dims, preferred_element_type=F32)
    else:
        dg = lambda p: lax.dot_general(p, e, dims, preferred_element_type=F32)
    hi, mid, lo = _split3(x)
    return dg(hi) + dg(mid) + dg(lo)


def _params(sem):
    return pltpu.CompilerParams(dimension_semantics=sem, vmem_limit_bytes=VMEM_LIMIT)


def _ada_kernel(c_ref, w_ref, b_ref, o_ref):
    c = c_ref[...]
    s = c * _sigmoid(c)
    o_ref[0] = _dot(s.astype(BF16), w_ref[0].astype(BF16)) + b_ref[0]


def _ada(c_all, w_ada, b_ada):
    depth, d, n = w_ada.shape
    r = c_all.shape[0]
    tn = 1024
    return pl.pallas_call(
        _ada_kernel,
        grid=(depth, n // tn),
        in_specs=[pl.BlockSpec((r, d), lambda l, j: (0, 0)),
                  pl.BlockSpec((1, d, tn), lambda l, j: (l, 0, j)),
                  pl.BlockSpec((1, 1, tn), lambda l, j: (l, 0, j))],
        out_specs=pl.BlockSpec((1, r, tn), lambda l, j: (l, 0, j)),
        out_shape=jax.ShapeDtypeStruct((depth, r, n), F32),
        compiler_params=_params(("parallel", "parallel")),
        name="ada",
    )(c_all, w_ada, b_ada.reshape(depth, 1, n))


def _rms_mod(x, g, sc, sh):
    ms = jnp.mean(x * x, axis=-1, keepdims=True)
    h = x * lax.rsqrt(ms + NORM_EPS) * g
    return h * (1.0 + sc) + sh


def _inproj_kernel(x_ref, g_ref, sc_ref, sh_ref, w_ref, o_ref, *, nchunk):
    h = _rms_mod(x_ref[0], g_ref[...], sc_ref[0], sh_ref[0]).astype(BF16)
    for n in range(NP // nchunk):
        o_ref[0, :, n * nchunk:(n + 1) * nchunk] = _dot(h, w_ref[:, n * nchunk:(n + 1) * nchunk])


def _mod_spec(mod, tm, d, ngrid):
    per_row = mod.shape[1] != 1
    if ngrid == 2:
        imap = (lambda b, i: (b, i, 0)) if per_row else (lambda b, i: (b, 0, 0))
    else:
        imap = (lambda b, i, f: (b, i, 0)) if per_row else (lambda b, i, f: (b, 0, 0))
    return pl.BlockSpec((1, tm if per_row else 1, d), imap)


def _inproj(x, g, sc, sh, w, tm):
    bn, t, d = x.shape
    return pl.pallas_call(
        functools.partial(_inproj_kernel, nchunk=1024),
        grid=(bn, t // tm),
        in_specs=[pl.BlockSpec((1, tm, d), lambda b, i: (b, i, 0)),
                  pl.BlockSpec((1, d), lambda b, i: (0, 0)),
                  _mod_spec(sc, tm, d, 2),
                  _mod_spec(sh, tm, d, 2),
                  pl.BlockSpec((d, NP), lambda b, i: (0, 0))],
        out_specs=pl.BlockSpec((1, tm, NP), lambda b, i: (b, i, 0)),
        out_shape=jax.ShapeDtypeStruct((bn, t, NP), F32),
        compiler_params=_params(("parallel", "parallel")),
        name="inproj",
    )(x, g.reshape(1, d), sc, sh, w)


def _tail_kernel(x_ref, ya_ref, yb_ref, yc_ref, gt1_ref, sh2_ref, sc2_ref, gt2_ref, g2_ref, gf_ref,
                 wo_ref, wg_ref, wu_ref, wd_ref, o_ref, x1_s, h_s, acc_s, *, nf, final_norm):
    f = pl.program_id(2)

    @pl.when(f == 0)
    def _():
        mix = (_dot(ya_ref[0].astype(BF16), wo_ref[0:A_WIDTH])
               + _dot(yb_ref[0].astype(BF16), wo_ref[A_WIDTH:A_WIDTH + B_WIDTH])
               + _dot(yc_ref[0].astype(BF16), wo_ref[A_WIDTH + B_WIDTH:A_WIDTH + B_WIDTH + C_WIDTH]))
        x1 = x_ref[0] + gt1_ref[0] * mix
        x1_s[...] = x1
        h_s[...] = _rms_mod(x1, g2_ref[...], sc2_ref[0], sh2_ref[0]).astype(BF16)
        acc_s[...] = jnp.zeros_like(acc_s)

    hb = h_s[...]
    gate = _dot(hb, wg_ref[...])
    up = _dot(hb, wu_ref[...])
    act = (gate * _sigmoid(gate)) * up
    acc_s[...] += _dot(act.astype(BF16), wd_ref[...])

    @pl.when(f == nf - 1)
    def _():
        y = x1_s[...] + gt2_ref[0] * acc_s[...]
        if final_norm:
            ms = jnp.mean(y * y, axis=-1, keepdims=True)
            y = y * lax.rsqrt(ms + NORM_EPS) * gf_ref[...]
        o_ref[0] = y


def _tail(x, ya, yb, yc, gt1, sh2, sc2, gt2, g2, gf, wo, wgu, wd, tm, tf, final_norm):
    bn, t, d = x.shape
    dff = wd.shape[0]
    nf = dff // tf
    row = lambda w: pl.BlockSpec((1, tm, w), lambda b, i, f: (b, i, 0))
    vec = pl.BlockSpec((1, d), lambda b, i, f: (0, 0))
    return pl.pallas_call(
        functools.partial(_tail_kernel, nf=nf, final_norm=final_norm),
        grid=(bn, t // tm, nf),
        in_specs=[row(d), row(A_WIDTH), row(B_WIDTH), row(C_WIDTH),
                  _mod_spec(gt1, tm, d, 3), _mod_spec(sh2, tm, d, 3), _mod_spec(sc2, tm, d, 3),
                  _mod_spec(gt2, tm, d, 3), vec, vec,
                  pl.BlockSpec((d, d), lambda b, i, f: (0, 0)),
                  pl.BlockSpec((d, tf), lambda b, i, f: (0, f)),
                  pl.BlockSpec((d, tf), lambda b, i, f: (0, f + nf)),
                  pl.BlockSpec((tf, d), lambda b, i, f: (f, 0))],
        out_specs=pl.BlockSpec((1, tm, d), lambda b, i, f: (b, i, 0)),
        out_shape=jax.ShapeDtypeStruct((bn, t, d), F32),
        scratch_shapes=[pltpu.VMEM((tm, d), F32), pltpu.VMEM((tm, d), BF16), pltpu.VMEM((tm, d), F32)],
        compiler_params=_params(("parallel", "parallel", "arbitrary")),
        name="tail",
    )(x, ya, yb, yc, gt1, sh2, sc2, gt2, g2.reshape(1, d), gf.reshape(1, d), wo, wgu, wgu, wd)


def _unit_lower_inverses(mats, nil, mm, pre):
    n = mats[0].shape[0]
    eye = (_iota2((n, n), 0) == _iota2((n, n), 1)).astype(F32)
    p = [-a for a in mats]
    t = [eye + m for m in p]
    k = 2
    while k < nil:
        ps = [pre(x) for x in p]
        p = [mm(x, x) for x in ps]
        t = [x + mm(x, y) for x, y in zip(t, p)]
        k *= 2
    return t


def _tri_masks(c):
    rows = _iota2((c, c), 0)
    cols = _iota2((c, c), 1)
    return rows >= cols, rows > cols, rows <= cols


def _block_tri_masks(nh, c):
    n = nh * c
    rows = _iota2((n, n), 0)
    cols = _iota2((n, n), 1)
    same = (rows // c) == (cols // c)
    return jnp.logical_and(same, rows >= cols), jnp.logical_and(same, rows > cols)


def _head_blocks(x, nh):
    t, n = x.shape
    w = n // nh
    tiled = jnp.concatenate([x] * nh, axis=0)
    keep = (_iota2((nh * t, n), 0) // t) == (_iota2((nh * t, n), 1) // w)
    return jnp.where(keep, tiled, jnp.zeros_like(tiled))


def _sum_row_blocks(x, nh):
    t = x.shape[0] // nh
    out = x[0:t]
    for h in range(1, nh):
        out = out + x[h * t:(h + 1) * t]
    return out


def _scan_chunks(lg_ref, bh_ref, st_s, y_s, c, ncb, heads, dv):
    s = [st_s[h] for h in range(heads)]
    for ck in range(ncb):
        for h in range(heads):
            s_s = _split2(s[h])
            y_s[ck * c:(ck + 1) * c, h * dv:(h + 1) * dv] = (
                _dot(lg_ref[0, ck, h, 0:c].astype(BF16), s_s[0]) + bh_ref[0, ck, h, 0:c])
            s[h] = _mm(lg_ref[0, ck, h, c:], s_s) + bh_ref[0, ck, h, c:]
    for h in range(heads):
        st_s[h] = s[h]


def _scan_specs(heads, c, dk, dv, ncb):
    return [pl.BlockSpec((1, ncb, heads, c + dk, dk), lambda b, i: (b, i, 0, 0, 0)),
            pl.BlockSpec((1, ncb, heads, c + dk, dv), lambda b, i: (b, i, 0, 0, 0)),
            pl.BlockSpec((1, heads, dk, dv), lambda b, i: (b, 0, 0, 0))]


def _prev_rows(proj, t, rb, width, colblk):
    bn = proj.shape[0]
    if t == rb:
        return jnp.zeros((bn, SUB, width), F32), pl.BlockSpec((1, SUB, width), lambda b, i: (b, 0, 0))
    return proj, pl.BlockSpec((1, SUB, width),
                              lambda b, i: (b, jnp.maximum(i * (rb // SUB) - 1, 0), colblk))


def _rwkv_a_kernel(p_ref, pp_ref, sh0_ref, mu_ref, w0_ref, w2_ref, a0_ref, a2_ref, g2_ref, kk_ref, ka_ref,
                   rk_ref, lg_ref, bh_ref, eg_ref, ext_s, *, tb, c, nck):
    i = pl.program_id(1)
    nw = A_WIDTH
    rc = nck * c
    rb = nck * tb

    prev = jnp.where(i == 0, sh0_ref[0], pp_ref[0, SUB - 1:SUB, :])
    ext_s[SUB - 1:SUB] = prev
    ext_s[SUB:SUB + rb] = p_ref[0]
    if tb < c:
        ext_s[SUB + rb:SUB + rc] = jnp.zeros((rc - rb, ext_s.shape[1]), F32)
    p = ext_s[SUB:SUB + rc]
    p_prev = ext_s[SUB - 1:SUB - 1 + rc]
    m = p + (p_prev - p) * mu_ref[...]

    r = m[:, 0:nw]
    k = m[:, nw:2 * nw]
    v = m[:, 2 * nw:3 * nw]
    pw = m[:, 3 * nw:3 * nw + 64]
    pa = m[:, 3 * nw + 64:3 * nw + 128]
    pg = m[:, 3 * nw + 128:3 * nw + 256]

    w = -_softplus(-(w0_ref[...] + _mm1(jnp.tanh(pw), w2_ref[...]))) - 0.5
    logd = -jnp.exp(w)
    a = _sigmoid(a0_ref[...] + _mm1(pa, a2_ref[...]))
    g = _mm1(_sigmoid(pg), g2_ref[...])

    ones_blk = ((_iota2((nw, nw), 0) // A_DIM) == (_iota2((nw, nw), 1) // A_DIM)).astype(BF16)
    hsum = lambda t: _mm01(t, ones_blk)

    kkf = k * kk_ref[...]
    kk = kkf / jnp.maximum(jnp.sqrt(hsum(kkf * kkf)), 1e-12)
    k2 = k * (1.0 + (a - 1.0) * ka_ref[...])

    if tb < c:
        valid = _iota2((rc, nw), 0) < tb
        zero = jnp.zeros((rc, nw), F32)
        logd = jnp.where(valid, logd, zero)
        kk = jnp.where(valid, kk, zero)
        k2 = jnp.where(valid, k2, zero)
        v = jnp.where(valid, v, zero)

    bonus = hsum(r * k2 * rk_ref[...]) * v
    eg_ref[0, :, 0:nw] = bonus[0:rb]
    eg_ref[0, :, nw:2 * nw] = g[0:rb]

    lincl = _tri_masks(c)[0].astype(BF16)
    nh = A_HEADS
    hc = nh * c
    incl, strict = _block_tri_masks(nh, c)
    zb = jnp.zeros((hc, hc), F32)
    eye = (_iota2((nw, nw), 0) == _iota2((nw, nw), 1)).astype(F32)
    each = lambda f, *lists: [f(*xs) for xs in zip(*lists)]
    rs = [slice(ck * c, (ck + 1) * c) for ck in range(nck)]
    logd_c = [logd[s] for s in rs]
    cum = each(lambda x: _mm01(x, lincl, e_left=True), logd_c)
    e_in = each(jnp.exp, cum)
    e_ex = each(lambda x, y: jnp.exp(x - y), cum, logd_c)
    e_neg = each(lambda x: jnp.exp(-x), cum)
    pc = [x[c - 1:c] for x in e_in]
    kkt = each(lambda s, e: _head_blocks(kk[s] * e, nh), rs, e_ex)
    rt = each(lambda s, e: _head_blocks(r[s] * e, nh), rs, e_in)
    kah = each(lambda s, e: _head_blocks(kk[s] * a[s] * e, nh), rs, e_neg)
    kh = each(lambda s, e: _head_blocks(k2[s] * e, nh), rs, e_neg)
    mm = _mm1
    b16 = lambda x: x.astype(BF16)
    v_s = each(lambda s: b16(_head_blocks(v[s], nh)), rs)
    pr = each(lambda x1, x2, y1, y2: mm(jnp.concatenate([x1, x2], axis=0), jnp.concatenate([y1, y2], axis=0),
                                        _NT), kkt, rt, kah, kh)
    la = [jnp.where(strict, x[0:hc, 0:hc], zb) for x in pr]
    lk = [jnp.where(strict, x[0:hc, hc:2 * hc], zb) for x in pr]
    ma = [jnp.where(incl, x[hc:2 * hc, 0:hc], zb) for x in pr]
    mk = [jnp.where(incl, x[hc:2 * hc, hc:2 * hc], zb) for x in pr]
    tinv = _unit_lower_inverses(la, c, mm, b16)
    lkv = each(mm, lk, v_s)
    w1b1 = each(lambda t, x, y: b16(-mm(t, jnp.concatenate([x, y], axis=1))), tinv, kkt, lkv)
    mw = each(mm, ma, w1b1)
    mkv = each(mm, mk, v_s)
    gh = each(lambda x, p, y: mm(x * p, y, _TN), kah, pc, w1b1)
    khv = each(lambda x, p, y: mm(x * p, y, _TN), kh, pc, v_s)
    for ck in range(nck):
        lg_ref[0, ck, 0:c] = _sum_row_blocks(rt[ck] + mw[ck][:, 0:nw], nh)
        lg_ref[0, ck, c:c + A_DIM] = _sum_row_blocks(eye * pc[ck] + gh[ck][:, 0:nw], nh)
        bh_ref[0, ck, 0:c] = _sum_row_blocks(mw[ck][:, nw:2 * nw] + mkv[ck], nh)
        bh_ref[0, ck, c:c + A_DIM] = _sum_row_blocks(gh[ck][:, nw:2 * nw] + khv[ck], nh)


def _rwkv_b_kernel(lg_ref, bh_ref, s0_ref, eg_ref, lw_ref, lb_ref, y_ref, sn_ref, st_s, y_s, *, tb, c, ncb):
    nw = A_WIDTH
    nh = A_HEADS

    @pl.when(pl.program_id(1) == 0)
    def _():
        st_s[...] = _head_blocks(s0_ref[0], nh)

    s = st_s[...]
    rbk = c + A_DIM
    for ck in range(ncb):
        s_s = _split2(s)
        y_blocks = _dot(_head_blocks(lg_ref[0, ck, 0:c], nh).astype(BF16), s_s[0])
        y_s[ck * c:(ck + 1) * c] = bh_ref[0, ck, 0:c] + _sum_row_blocks(y_blocks, nh)
        s = _mm(_head_blocks(lg_ref[0, ck, c:rbk], nh), s_s) + _head_blocks(bh_ref[0, ck, c:rbk], nh)
    st_s[...] = s
    rb = ncb * tb
    y = y_s[...]
    ones_blk = ((_iota2((nw, nw), 0) // A_DIM) == (_iota2((nw, nw), 1) // A_DIM)).astype(BF16)
    hsum = lambda t: _mm01(t, ones_blk)
    mean = hsum(y) * (1.0 / A_DIM)
    yc = y - mean
    var = hsum(yc * yc) * (1.0 / A_DIM)
    yn = yc * lax.rsqrt(var + A_LNX_EPS) * lw_ref[...] + lb_ref[...]
    eg = eg_ref[0]
    y_ref[0] = (yn[0:rb] + eg[:, 0:nw]) * eg[:, nw:2 * nw]
    sn_ref[0] = _sum_row_blocks(s, nh)


def _rwkv(proj, shift0, s0, lp, c, nck, ncb):
    bn, t, _ = proj.shape
    tb = min(t, c)
    nck = nck if tb == c else 1
    ncb = ncb if tb == c else 1
    nc = t // tb
    cols = 4 * A_WIDTH
    rb = nck * tb
    vec = lambda n: pl.BlockSpec((1, n), lambda b, i: (0, 0))
    mat = lambda r, n: pl.BlockSpec((r, n), lambda b, i: (0, 0))
    row = lambda a: a.reshape(1, -1)
    pp, pp_spec = _prev_rows(proj, t, rb, cols, COL_A // cols)
    lg_shape = jax.ShapeDtypeStruct((bn, nc, c + A_DIM, A_WIDTH), F32)
    lg_spec = pl.BlockSpec((1, nck, c + A_DIM, A_WIDTH), lambda b, i: (b, i, 0, 0))
    lg, bh, eg = pl.pallas_call(
        functools.partial(_rwkv_a_kernel, tb=tb, c=c, nck=nck),
        grid=(bn, nc // nck),
        in_specs=[pl.BlockSpec((1, rb, cols), lambda b, i: (b, i, COL_A // cols)),
                  pp_spec,
                  pl.BlockSpec((1, 1, cols), lambda b, i: (b, 0, 0)),
                  vec(cols), vec(A_WIDTH), mat(64, A_WIDTH), vec(A_WIDTH), mat(64, A_WIDTH), mat(128, A_WIDTH),
                  vec(A_WIDTH), vec(A_WIDTH), vec(A_WIDTH)],
        out_specs=[lg_spec, lg_spec, pl.BlockSpec((1, rb, 2 * A_WIDTH), lambda b, i: (b, i, 0))],
        out_shape=[lg_shape, lg_shape, jax.ShapeDtypeStruct((bn, t, 2 * A_WIDTH), F32)],
        scratch_shapes=[pltpu.VMEM((SUB + nck * c, cols), F32)],
        compiler_params=_params(("parallel", "parallel")),
        name="rwkv_a",
    )(proj, pp, shift0.reshape(bn, 1, cols), row(lp["a_mu"]), row(lp["a_w0"]), lp["a_w2"], row(lp["a_a0"]),
      lp["a_a2"], lp["a_g2"], row(lp["a_k_k"]), row(lp["a_k_a"]), row(lp["a_r_k"]))
    y, st = pl.pallas_call(
        functools.partial(_rwkv_b_kernel, tb=tb, c=c, ncb=ncb),
        grid=(bn, nc // ncb),
        in_specs=[pl.BlockSpec((1, ncb, c + A_DIM, A_WIDTH), lambda b, i: (b, i, 0, 0)),
                  pl.BlockSpec((1, ncb, c + A_DIM, A_WIDTH), lambda b, i: (b, i, 0, 0)),
                  pl.BlockSpec((1, A_DIM, A_WIDTH), lambda b, i: (b, 0, 0)),
                  pl.BlockSpec((1, ncb * tb, 2 * A_WIDTH), lambda b, i: (b, i, 0)), vec(A_WIDTH), vec(A_WIDTH)],
        out_specs=[pl.BlockSpec((1, ncb * tb, A_WIDTH), lambda b, i: (b, i, 0)),
                   pl.BlockSpec((1, A_DIM, A_WIDTH), lambda b, i: (b, 0, 0))],
        out_shape=[jax.ShapeDtypeStruct((bn, t, A_WIDTH), F32),
                   jax.ShapeDtypeStruct((bn, A_DIM, A_WIDTH), F32)],
        scratch_shapes=[pltpu.VMEM((A_WIDTH, A_WIDTH), F32), pltpu.VMEM((ncb * c, A_WIDTH), F32)],
        compiler_params=_params(("parallel", "arbitrary")),
        name="rwkv_b",
    )(lg, bh, jnp.transpose(s0, (0, 3, 1, 2)).reshape(bn, A_DIM, A_WIDTH), eg,
      row(lp["a_lnx_w"]), row(lp["a_lnx_b"]))
    return y, jnp.transpose(st.reshape(bn, A_DIM, A_HEADS, A_DIM), (0, 2, 3, 1))


def _gdn_a_kernel(qkv_ref, qp_ref, bg_ref, cv0_ref, cw_ref, alog_ref, dtb_ref, lg_ref, bh_ref, ext_s, bg_s,
                  *, tb, c, nck):
    i = pl.program_id(1)
    rc = nck * c
    rb = nck * tb

    ext_s[0:SUB] = jnp.where(i == 0, cv0_ref[0], qp_ref[0])
    ext_s[SUB:SUB + rb] = qkv_ref[0]
    if tb < c:
        ext_s[SUB + rb:SUB + rc] = jnp.zeros((rc - rb, ext_s.shape[1]), F32)
    cw = cw_ref[...]
    u = ext_s[SUB - 3:SUB - 3 + rc] * cw[0:1]
    for j in range(1, B_CONV):
        u = u + ext_s[SUB - 3 + j:SUB - 3 + j + rc] * cw[j:j + 1]
    qkv = u * _sigmoid(u)

    nb = bg_s.shape[1]
    bg_s[0:rb] = bg_ref[0]
    if tb < c:
        bg_s[rb:rc] = jnp.zeros((rc - rb, nb), F32)
    bg = bg_s[...]
    beta_all = _sigmoid(bg)
    gl = -jnp.exp(alog_ref[...]) * _softplus(bg + dtb_ref[...])
    if tb < c:
        valid = _iota2((rc, nb), 0) < tb
        beta_all = jnp.where(valid, beta_all, jnp.zeros_like(beta_all))
        gl = jnp.where(valid, gl, jnp.zeros_like(gl))
    gl = gl[:, 0:LANE]

    lincl, _, upper = _tri_masks(c)
    nh = B_HEADS
    hc = nh * c
    incl, strict = _block_tri_masks(nh, c)
    zb = jnp.zeros((hc, hc), F32)
    own_head = (_iota2((hc, B_WIDTH), 0) // c) == (_iota2((hc, B_WIDTH), 1) // B_DIM)
    eye_rows = (_iota2((B_WIDTH, B_DIM), 0) % B_DIM) == _iota2((B_WIDTH, B_DIM), 1)
    each = lambda f, *lists: [f(*xs) for xs in zip(*lists)]
    rs = [slice(ck * c, (ck + 1) * c) for ck in range(nck)]
    heads_on_rows = lambda s, col0: jnp.concatenate(
        [qkv[s, col0 + h * B_DIM:col0 + (h + 1) * B_DIM] for h in range(nh)], axis=0)
    l2n = lambda x: x * lax.rsqrt(jnp.sum(x * x, axis=-1, keepdims=True) + L2_EPS)
    gc = each(lambda s: _mm01(gl[s], lincl.astype(BF16), e_left=True), rs)
    gct = each(lambda s: _mm01(gl[s], upper.astype(BF16), _TN), rs)
    q_r = each(lambda s: l2n(heads_on_rows(s, 0)) * (B_DIM ** -0.5), rs)
    k_r = each(lambda s: l2n(heads_on_rows(s, B_WIDTH)), rs)
    v_r = each(lambda s: heads_on_rows(s, 2 * B_WIDTH), rs)
    gcol = [jnp.concatenate([x[:, nh + h:nh + h + 1] for h in range(nh)], axis=0) for x in gc]
    grow = [jnp.concatenate([x[nh + h:nh + h + 1, :] for h in range(nh)], axis=1) for x in gct]
    bcol = [jnp.concatenate([beta_all[s, h:h + 1] for h in range(nh)], axis=0) for s in rs]
    glast = [jnp.concatenate([jnp.broadcast_to(x[c - 1:c, nh + h:nh + h + 1], (c, 1)) for h in range(nh)],
                             axis=0) for x in gc]
    eglast = [jnp.concatenate([jnp.broadcast_to(jnp.exp(x[c - 1:c, nh + h:nh + h + 1]), (B_DIM, 1))
                               for h in range(nh)], axis=0) for x in gc]
    decay = each(lambda x, y: jnp.where(incl, jnp.exp(jnp.where(incl, x - y, zb)), zb), gcol, grow)
    kb = each(lambda x, y: x * y, k_r, bcol)
    k_s = each(_split2, k_r)
    amat = each(lambda x, y, d: jnp.where(strict, _mm(x, y, _NT) * d, zb), kb, k_s, decay)
    qk = each(lambda x, y, d: _mm(x, y, _NT) * d, q_r, k_s, decay)
    tinv = _unit_lower_inverses(amat, c, _mm, _split2)
    eg = each(jnp.exp, gcol)
    uu = each(lambda t, x, b, y, e: _split2(_mm(t, jnp.concatenate([x * b, y * e], axis=1))),
              tinv, v_r, bcol, kb, eg)
    qu = each(_mm, qk, uu)
    kd_blk = each(lambda x, gl_, gc_: jnp.where(own_head, jnp.concatenate([x * jnp.exp(gl_ - gc_)] * nh, axis=1),
                                                jnp.zeros((hc, B_WIDTH), F32)), k_r, glast, gcol)
    ku = each(lambda x, y: _mm(x, y, _TN), kd_blk, uu)
    for ck in range(nck):
        w2 = q_r[ck] * eg[ck] - qu[ck][:, B_DIM:]
        g = jnp.where(eye_rows, eglast[ck], jnp.zeros((B_WIDTH, B_DIM), F32)) - ku[ck][:, B_DIM:]
        for h in range(nh):
            lg_ref[0, ck, h, 0:c] = w2[h * c:(h + 1) * c]
            lg_ref[0, ck, h, c:c + B_DIM] = g[h * B_DIM:(h + 1) * B_DIM]
            bh_ref[0, ck, h, 0:c] = qu[ck][h * c:(h + 1) * c, 0:B_DIM]
            bh_ref[0, ck, h, c:c + B_DIM] = ku[ck][h * B_DIM:(h + 1) * B_DIM, 0:B_DIM]


def _gdn_b_kernel(lg_ref, bh_ref, s0_ref, z_ref, nw_ref, y_ref, sn_ref, st_s, y_s, *, tb, c, ncb):
    @pl.when(pl.program_id(1) == 0)
    def _():
        st_s[...] = s0_ref[0]

    _scan_chunks(lg_ref, bh_ref, st_s, y_s, c, ncb, B_HEADS, B_DIM)
    rb = ncb * tb
    z = z_ref[0]
    for h in range(B_HEADS):
        sl = slice(h * B_DIM, (h + 1) * B_DIM)
        o = y_s[:, sl]
        on = o * lax.rsqrt(jnp.mean(o * o, axis=-1, keepdims=True) + NORM_EPS) * nw_ref[...]
        z_h = z[:, sl]
        y_ref[0, :, sl] = on[0:rb] * (z_h * _sigmoid(z_h))
    sn_ref[0] = st_s[...]


def _gdn(proj, conv0, s0, lp, c, nck, ncb):
    bn, t, _ = proj.shape
    tb = min(t, c)
    nck = nck if tb == c else 1
    ncb = ncb if tb == c else 1
    nc = t // tb
    rb = nck * tb
    nb = NP - COL_BG
    cv0 = jnp.concatenate([jnp.zeros((bn, SUB - (B_CONV - 1), B_QKV), F32), conv0], axis=1)
    lane8 = lambda a: jnp.zeros((1, nb), F32).at[0, B_HEADS:2 * B_HEADS].set(a)
    qp, qp_spec = _prev_rows(proj, t, rb, B_QKV, COL_BQKV // B_QKV)
    lg_shape = jax.ShapeDtypeStruct((bn, nc, B_HEADS, c + B_DIM, B_DIM), F32)
    lg_spec = pl.BlockSpec((1, nck, B_HEADS, c + B_DIM, B_DIM), lambda b, i: (b, i, 0, 0, 0))
    lg, bh = pl.pallas_call(
        functools.partial(_gdn_a_kernel, tb=tb, c=c, nck=nck),
        grid=(bn, nc // nck),
        in_specs=[pl.BlockSpec((1, rb, B_QKV), lambda b, i: (b, i, COL_BQKV // B_QKV)),
                  qp_spec,
                  pl.BlockSpec((1, rb, nb), lambda b, i: (b, i, COL_BG // nb)),
                  pl.BlockSpec((1, SUB, B_QKV), lambda b, i: (b, 0, 0)),
                  pl.BlockSpec((B_CONV, B_QKV), lambda b, i: (0, 0)),
                  pl.BlockSpec((1, nb), lambda b, i: (0, 0)),
                  pl.BlockSpec((1, nb), lambda b, i: (0, 0))],
        out_specs=[lg_spec, lg_spec],
        out_shape=[lg_shape, lg_shape],
        scratch_shapes=[pltpu.VMEM((SUB + nck * c, B_QKV), F32), pltpu.VMEM((nck * c, nb), F32)],
        compiler_params=_params(("parallel", "parallel")),
        name="gdn_a",
    )(proj, qp, proj, cv0, lp["b_conv_w"], lane8(lp["b_a_log"]), lane8(lp["b_dt_bias"]))
    return pl.pallas_call(
        functools.partial(_gdn_b_kernel, tb=tb, c=c, ncb=ncb),
        grid=(bn, nc // ncb),
        in_specs=_scan_specs(B_HEADS, c, B_DIM, B_DIM, ncb)
                 + [pl.BlockSpec((1, ncb * tb, B_WIDTH), lambda b, i: (b, i, COL_BZ // B_WIDTH)),
                    pl.BlockSpec((1, B_DIM), lambda b, i: (0, 0))],
        out_specs=[pl.BlockSpec((1, ncb * tb, B_WIDTH), lambda b, i: (b, i, 0)),
                   pl.BlockSpec((1, B_HEADS, B_DIM, B_DIM), lambda b, i: (b, 0, 0, 0))],
        out_shape=[jax.ShapeDtypeStruct((bn, t, B_WIDTH), F32),
                   jax.ShapeDtypeStruct((bn, B_HEADS, B_DIM, B_DIM), F32)],
        scratch_shapes=[pltpu.VMEM((B_HEADS, B_DIM, B_DIM), F32), pltpu.VMEM((ncb * c, B_WIDTH), F32)],
        compiler_params=_params(("parallel", "arbitrary")),
        name="gdn_b",
    )(lg, bh, s0, proj, lp["b_norm_w"].reshape(1, B_DIM))


def _sbp_kernel(bias_ref, q_ref, k_ref, v_ref, o_ref, carry_s, acc_s, *, tq, tk):
    qi = pl.program_id(1)
    j = pl.program_id(2)
    nsub = tq // tk

    @pl.when(j == 0)
    def _():
        carry_s[...] = jnp.zeros_like(carry_s)
        acc_s[...] = jnp.zeros_like(acc_s)

    def sweep(diag):
        qb = (q_ref[0] * (C_DIM ** -0.5 * LOG2E)).astype(BF16)
        kb = k_ref[0].astype(BF16)
        vb = v_ref[0].astype(BF16)
        tri = (_iota2((tk, 2 * tk), 1) % tk >= _iota2((tk, 2 * tk), 0)).astype(BF16)
        hsl = lambda h: slice(h * C_DIM, (h + 1) * C_DIM)
        tiles = [(s, h) for s in reversed(range(nsub)) for h in range(C_HEADS)]
        q0 = {s: s * tk if diag else 0 for s in range(nsub)}
        valid = {s: _iota2((tk, tq - q0[s]), 0) < _iota2((tk, tq - q0[s]), 1) for s in range(nsub)}
        zt = {(s, h): _dot_nt(kb[s * tk:(s + 1) * tk, hsl(h)], qb[q0[s]:, hsl(h)]) + bias_ref[h] * LOG2E
              for s, h in tiles}
        rc = {}
        for s, h in tiles:
            z = zt[s, h]
            sp = jnp.maximum(z, 0.0) + jnp.log2(1.0 + jnp.exp2(-jnp.abs(z)))
            if diag:
                sp = jnp.where(valid[s], sp, jnp.zeros_like(sp))
            hi, lo = _split2(sp)
            rc[s, h] = _dot(tri, jnp.concatenate([hi, lo], axis=0))
        carry = [carry_s[h:h + 1, :] for h in range(C_HEADS)]
        acc = [acc_s[h] for h in range(C_HEADS)]
        for s, h in tiles:
            at = jnp.exp2(zt[s, h] - rc[s, h] - carry[h][:, q0[s]:])
            if diag:
                at = jnp.where(valid[s], at, jnp.zeros_like(at))
            o_t = _dot_tn(vb[s * tk:(s + 1) * tk, hsl(h)], at.astype(BF16))
            c_t = rc[s, h][0:1]
            if q0[s]:
                o_t = jnp.concatenate([jnp.zeros((C_DIM, q0[s]), F32), o_t], axis=1)
                c_t = jnp.concatenate([jnp.zeros((1, q0[s]), F32), c_t], axis=1)
            acc[h] = acc[h] + o_t
            carry[h] = carry[h] + c_t
        for h in range(C_HEADS):
            carry_s[h:h + 1, :] = carry[h]
            acc_s[h] = acc[h]

    @pl.when(j == 0)
    def _():
        sweep(True)

    @pl.when(jnp.logical_and(j > 0, j <= qi))
    def _():
        sweep(False)

    @pl.when(j == qi)
    def _():
        o_ref[0] = acc_s[...].reshape(C_WIDTH, tq).T


def _sb_prompt(proj, bias, tq, tk):
    bn, t, _ = proj.shape
    nq = t // tq
    qcol = COL_C // C_WIDTH
    return pl.pallas_call(
        functools.partial(_sbp_kernel, tq=tq, tk=tk),
        grid_spec=pltpu.PrefetchScalarGridSpec(
            num_scalar_prefetch=0,
            grid=(bn, nq, nq),
            in_specs=[pl.BlockSpec(memory_space=pltpu.SMEM),
                      pl.BlockSpec((1, tq, C_WIDTH), lambda b, i, j: (b, i, qcol)),
                      pl.BlockSpec((1, tq, C_WIDTH), lambda b, i, j: (b, jnp.maximum(i - j, 0), qcol + 1)),
                      pl.BlockSpec((1, tq, C_WIDTH), lambda b, i, j: (b, jnp.maximum(i - j, 0), qcol + 2))],
            out_specs=pl.BlockSpec((1, tq, C_WIDTH), lambda b, i, j: (b, i, 0)),
            scratch_shapes=[pltpu.VMEM((C_HEADS, tq), F32), pltpu.VMEM((C_HEADS, C_DIM, tq), F32)]),
        out_shape=jax.ShapeDtypeStruct((bn, t, C_WIDTH), F32),
        compiler_params=_params(("parallel", "parallel", "arbitrary")),
        name="sbp",
    )(bias, proj, proj, proj)


def _sbs_kernel(pt_ref, bias_ref, q_ref, k_ref, v_ref, *rest, tn, npage, ps):
    kp_refs = rest[0:npage]
    vp_refs = rest[npage:2 * npage]
    o_ref, q_s, carry_s, acc_s, kown_s, vown_s = rest[2 * npage:]
    j = pl.program_id(1)
    nj = pl.num_programs(1)
    rows = C_HEADS * SUB
    tri = (_iota2((ps, ps), 0) >= _iota2((ps, ps), 1)).astype(BF16)
    bias_col = jnp.concatenate([jnp.full((SUB, 1), bias_ref[h], F32) for h in range(C_HEADS)], axis=0)
    hsl = lambda h: slice(h * C_DIM, (h + 1) * C_DIM)
    rsl = lambda h: slice(h * SUB, (h + 1) * SUB)

    @pl.when(j == 0)
    def _():
        q_s[...] = jnp.zeros_like(q_s)
        q_s[0:tn] = q_ref[0] * (C_DIM ** -0.5)
        kown_s[...] = jnp.zeros_like(kown_s)
        vown_s[...] = jnp.zeros_like(vown_s)
        kown_s[0:tn] = k_ref[0]
        vown_s[0:tn] = v_ref[0]
        qb = q_s[...].astype(BF16)
        kb = kown_s[...].astype(BF16)
        vb = vown_s[...].astype(BF16)
        z = jnp.concatenate([_dot_nt(qb[:, hsl(h)], kb[:, hsl(h)]) for h in range(C_HEADS)], axis=0) + bias_col
        valid = _iota2((rows, ps), 1) < (_iota2((rows, ps), 0) % SUB)
        zero = jnp.zeros((rows, ps), F32)
        rc = _mm01(jnp.where(valid, _softplus(z), zero), tri)
        a = jnp.where(valid, jnp.exp(z - rc), zero)
        for h in range(C_HEADS):
            acc_s[h] = _dot(a[rsl(h)].astype(BF16), vb[:, hsl(h)])
        carry_s[...] = rc[:, 0:1]

    qb = q_s[...].astype(BF16)
    kt = [kp_refs[i][0, 0].astype(BF16) for i in range(npage)]
    vt = [vp_refs[i][0, 0].astype(BF16) for i in range(npage)]
    z = jnp.concatenate(
        [_dot(qb[:, hsl(h)], jnp.concatenate([kt[i][h] for i in range(npage)], axis=1)) for h in range(C_HEADS)],
        axis=0) + bias_col
    sp = _softplus(z)
    sp_rows = jnp.concatenate([sp[:, i * ps:(i + 1) * ps] for i in range(npage)], axis=0)
    hi, lo = _split2(sp_rows)
    rc_rows = _dot(jnp.concatenate([hi, lo], axis=1), jnp.concatenate([tri, tri], axis=0))
    carry = carry_s[...]
    a_segs = []
    for i in range(npage):
        seg = slice(i * ps, (i + 1) * ps)
        rc = rc_rows[i * rows:(i + 1) * rows]
        a_segs.append(jnp.exp(z[:, seg] - rc - carry))
        carry = carry + rc[:, 0:1]
    carry_s[...] = carry
    a = jnp.concatenate(a_segs, axis=1)
    for h in range(C_HEADS):
        v_h = jnp.concatenate([vt[i][h] for i in range(npage)], axis=1)
        acc_s[h] += _dot_nt(a[rsl(h)].astype(BF16), v_h)

    @pl.when(j == nj - 1)
    def _():
        for h in range(C_HEADS):
            o_ref[0, :, hsl(h)] = acc_s[h][0:tn]


def _sb_sample(proj, cache_kt, cache_vt, page_table, bias, layer, npage):
    bn, tn, _ = proj.shape
    n_pages = page_table.shape[1]
    ps = cache_kt.shape[-1]
    qcol = COL_C // C_WIDTH
    page_spec = lambda i: pl.BlockSpec(
        (1, 1, C_HEADS, C_DIM, ps),
        lambda b, j, pt: (layer, pt[b, n_pages - 1 - (j * npage + i)], 0, 0, 0))
    own = lambda off: pl.BlockSpec((1, tn, C_WIDTH), lambda b, j, pt: (b, 0, qcol + off))
    return pl.pallas_call(
        functools.partial(_sbs_kernel, tn=tn, npage=npage, ps=ps),
        grid_spec=pltpu.PrefetchScalarGridSpec(
            num_scalar_prefetch=1,
            grid=(bn, n_pages // npage),
            in_specs=[pl.BlockSpec(memory_space=pltpu.SMEM), own(0), own(1), own(2)]
                     + [page_spec(i) for i in range(npage)] + [page_spec(i) for i in range(npage)],
            out_specs=pl.BlockSpec((1, tn, C_WIDTH), lambda b, j, pt: (b, 0, 0)),
            scratch_shapes=[pltpu.VMEM((SUB, C_WIDTH), F32),
                            pltpu.VMEM((C_HEADS * SUB, 1), F32),
                            pltpu.VMEM((C_HEADS, SUB, C_DIM), F32),
                            pltpu.VMEM((ps, C_WIDTH), F32),
                            pltpu.VMEM((ps, C_WIDTH), F32)]),
        out_shape=jax.ShapeDtypeStruct((bn, tn, C_WIDTH), F32),
        compiler_params=_params(("parallel", "arbitrary")),
        name="sbs",
    )(page_table, bias, proj, proj, proj, *([cache_kt] * npage), *([cache_vt] * npage))


def _reorder_w_in(w):
    a_cols = 4 * A_WIDTH
    b0 = a_cols
    bz0 = b0 + B_QKV
    bg0 = bz0 + B_WIDTH
    c0 = bg0 + 2 * B_HEADS
    c1 = c0 + 3 * C_WIDTH
    pad = jnp.zeros((w.shape[0], NP - (COL_BG + 2 * B_HEADS)), w.dtype)
    return jnp.concatenate([w[:, 0:a_cols], w[:, bz0:bg0], w[:, b0:bz0], w[:, c0:c1], w[:, bg0:c0], pad],
                           axis=1).astype(BF16)


def kernel(x_prompt, x_sample, cache_k, cache_v, state_shift, state_wkv, state_conv, state_gdn, page_table,
           c_prompt, c_sample, w_ada, b_ada, norm1, norm2, w_in, w_out, a_mu, a_w0, a_w2, a_a0, a_a2, a_g2,
           a_k_k, a_k_a, a_r_k, a_lnx_w, a_lnx_b, b_conv_w, b_a_log, b_dt_bias, b_norm_w, c_bias, w_gu, w_down,
           norm_f):
    depth = w_in.shape[0]
    nb, seq, d = x_prompt.shape
    nd, tn, _ = x_sample.shape
    n_ada = w_ada.shape[2] // d

    r_all = -(-(nb + nd) // SUB) * SUB
    c_all = jnp.concatenate([c_prompt, c_sample, jnp.zeros((r_all - nb - nd, d), F32)], axis=0)
    mod = _ada(c_all, w_ada, b_ada)

    cache_kt = jnp.transpose(cache_k, (0, 1, 3, 4, 2))
    cache_vt = jnp.transpose(cache_v, (0, 1, 3, 4, 2))

    xp = x_prompt
    xs = x_sample.reshape(1, nd * tn, d)
    outs_p = [[] for _ in range(6)]
    outs_s = [[] for _ in range(6)]
    for l in range(depth):
        lp = dict(a_mu=a_mu[l], a_w0=a_w0[l], a_w2=a_w2[l], a_a0=a_a0[l], a_a2=a_a2[l], a_g2=a_g2[l],
                  a_k_k=a_k_k[l], a_k_a=a_k_a[l], a_r_k=a_r_k[l], a_lnx_w=a_lnx_w[l], a_lnx_b=a_lnx_b[l],
                  b_conv_w=b_conv_w[l], b_a_log=b_a_log[l], b_dt_bias=b_dt_bias[l], b_norm_w=b_norm_w[l])
        w_in_l = _reorder_w_in(w_in[l])
        w_out_l = w_out[l].astype(BF16)
        w_gu_l = w_gu[l].astype(BF16)
        w_down_l = w_down[l].astype(BF16)
        last = l == depth - 1

        mods_p = [m.reshape(nb, 1, d) for m in jnp.split(mod[l, 0:nb], n_ada, axis=-1)]
        mods_s = [jnp.repeat(m, tn, axis=0).reshape(1, nd * tn, d)
                  for m in jnp.split(mod[l, nb:nb + nd], n_ada, axis=-1)]

        sh1, sc1, gt1, sh2, sc2, gt2 = mods_p
        proj = _inproj(xp, norm1[l], sc1, sh1, w_in_l, tm=256)
        ya, wkv_p = _rwkv(proj, jnp.zeros((nb, 4 * A_WIDTH), F32),
                          jnp.zeros((nb, A_HEADS, A_DIM, A_DIM), F32), lp, c=64, nck=4, ncb=8)
        yb, gdn_p = _gdn(proj, jnp.zeros((nb, B_CONV - 1, B_QKV), F32),
                         jnp.zeros((nb, B_HEADS, B_DIM, B_DIM), F32), lp, c=64, nck=4, ncb=8)
        yc = _sb_prompt(proj, c_bias[l], tq=512, tk=128)
        xp = _tail(xp, ya, yb, yc, gt1, sh2, sc2, gt2, norm2[l], norm_f, w_out_l, w_gu_l, w_down_l,
                   tm=512, tf=1408, final_norm=last)
        k_new = proj[:, :, COL_C + C_WIDTH:COL_C + 2 * C_WIDTH].reshape(nb, seq, C_HEADS, C_DIM)
        v_new = proj[:, :, COL_C + 2 * C_WIDTH:COL_C + 3 * C_WIDTH].reshape(nb, seq, C_HEADS, C_DIM)
        shift_new = proj[:, seq - 1, COL_A:COL_A + 4 * A_WIDTH]
        conv_new = proj[:, seq - (B_CONV - 1):, COL_BQKV:COL_BQKV + B_QKV]
        for lst, val in zip(outs_p, (k_new, v_new, shift_new, wkv_p, conv_new, gdn_p)):
            lst.append(val)

        sh1, sc1, gt1, sh2, sc2, gt2 = mods_s
        proj = _inproj(xs, norm1[l], sc1, sh1, w_in_l, tm=nd * tn).reshape(nd, tn, NP)
        ya, wkv_s = _rwkv(proj, state_shift[l], state_wkv[l], lp, c=SUB, nck=1, ncb=1)
        yb, gdn_s = _gdn(proj, state_conv[l], state_gdn[l], lp, c=SUB, nck=1, ncb=1)
        yc = _sb_sample(proj, cache_kt, cache_vt, page_table, c_bias[l], l, npage=32)
        flat = lambda a: a.reshape(1, nd * tn, a.shape[-1])
        xs = _tail(xs, flat(ya), flat(yb), flat(yc), gt1, sh2, sc2, gt2, norm2[l], norm_f, w_out_l, w_gu_l,
                   w_down_l, tm=nd * tn, tf=256, final_norm=last)
        k_new = proj[:, :, COL_C + C_WIDTH:COL_C + 2 * C_WIDTH].reshape(nd, tn, C_HEADS, C_DIM)
        v_new = proj[:, :, COL_C + 2 * C_WIDTH:COL_C + 3 * C_WIDTH].reshape(nd, tn, C_HEADS, C_DIM)
        shift_new = proj[:, tn - 1, COL_A:COL_A + 4 * A_WIDTH]
        conv_new = jnp.concatenate([state_conv[l], proj[:, :, COL_BQKV:COL_BQKV + B_QKV]],
                                   axis=1)[:, -(B_CONV - 1):]
        for lst, val in zip(outs_s, (k_new, v_new, shift_new, wkv_s, conv_new, gdn_s)):
            lst.append(val)

    y_prompt = xp
    y_sample = xs.reshape(nd, tn, d)
    p_out = tuple(jnp.stack(t) for t in outs_p)
    s_out = tuple(jnp.stack(t) for t in outs_s)
    return (y_prompt, y_sample) + p_out + s_out
```

```python
import functools

import jax
import jax.numpy as jnp
from jax import lax
from jax.experimental import pallas as pl
from jax.experimental.pallas import tpu as pltpu

F32 = jnp.float32
BF16 = jnp.bfloat16

LOG2E = 1.4426950408889634
NORM_EPS = 1e-6
L2_EPS = 1e-6
A_LNX_EPS = 64e-5

A_HEADS, A_DIM = 4, 64
A_WIDTH = A_HEADS * A_DIM
B_HEADS, B_DIM = 4, 128
B_WIDTH = B_HEADS * B_DIM
B_QKV = 3 * B_WIDTH
B_CONV = 4
C_HEADS, C_DIM = 4, 64
C_WIDTH = C_HEADS * C_DIM

NP = 4096
COL_A = 0
COL_BZ = 1024
COL_BQKV = 1536
COL_C = 3072
COL_BG = 3840

LANE = 128
SUB = 8
VMEM_LIMIT = 56 * 1024 * 1024

_NN = (((1,), (0,)), ((), ()))
_NT = (((1,), (1,)), ((), ()))
_TN = (((0,), (0,)), ((), ()))


def _dot(a, b):
    return lax.dot_general(a, b, _NN, preferred_element_type=F32)


def _dot_nt(a, b):
    return lax.dot_general(a, b, _NT, preferred_element_type=F32)


def _dot_tn(a, b):
    return lax.dot_general(a, b, _TN, preferred_element_type=F32)


def _iota2(shape, dim):
    return lax.broadcasted_iota(jnp.int32, shape, dim)


def _sigmoid(x):
    return jax.nn.sigmoid(x)


def _softplus(x):
    return jnp.maximum(x, 0.0) + jnp.log1p(jnp.exp(-jnp.abs(x)))


def _split2(x):
    hi = x.astype(BF16)
    return hi, (x - hi.astype(F32)).astype(BF16)


def _split3(x):
    hi = x.astype(BF16)
    r1 = x - hi.astype(F32)
    mid = r1.astype(BF16)
    lo = (r1 - mid.astype(F32)).astype(BF16)
    return hi, mid, lo


def _mm(a, b, dims=_NN):
    ah, al = a if isinstance(a, tuple) else _split2(a)
    bh, bl = b if isinstance(b, tuple) else _split2(b)
    dg = lambda x, y: lax.dot_general(x, y, dims, preferred_element_type=F32)
    return dg(ah, bh) + dg(ah, bl) + dg(al, bh)


def _mm1(a, b, dims=_NN):
    return lax.dot_general(a.astype(BF16), b.astype(BF16), dims, preferred_element_type=F32)


def _mm01(x, e, dims=_NN, e_left=False):
    if e_left:
        dg = lambda p: lax.dot_general(e, p, dims, preferred_element_type=F32)
    else:
        dg = lambda p: lax.dot_general(p, e, dims, preferred_element_type=F32)
    hi, mid, lo = _split3(x)
    return dg(hi) + dg(mid) + dg(lo)


def _params(sem):
    return pltpu.CompilerParams(dimension_semantics=sem, vmem_limit_bytes=VMEM_LIMIT)


def _ada_kernel(c_ref, w_ref, b_ref, o_ref):
    c = c_ref[...]
    s = c * _sigmoid(c)
    o_ref[0] = _dot(s.astype(BF16), w_ref[0].astype(BF16)) + b_ref[0]


def _ada(c_all, w_ada, b_ada):
    depth, d, n = w_ada.shape
    r = c_all.shape[0]
    tn = 1024
    return pl.pallas_call(
        _ada_kernel,
        grid=(depth, n // tn),
        in_specs=[pl.BlockSpec((r, d), lambda l, j: (0, 0)),
                  pl.BlockSpec((1, d, tn), lambda l, j: (l, 0, j)),
                  pl.BlockSpec((1, 1, tn), lambda l, j: (l, 0, j))],
        out_specs=pl.BlockSpec((1, r, tn), lambda l, j: (l, 0, j)),
        out_shape=jax.ShapeDtypeStruct((depth, r, n), F32),
        compiler_params=_params(("parallel", "parallel")),
        name="ada",
    )(c_all, w_ada, b_ada.reshape(depth, 1, n))


def _rms_mod(x, g, sc, sh):
    ms = jnp.mean(x * x, axis=-1, keepdims=True)
    h = x * lax.rsqrt(ms + NORM_EPS) * g
    return h * (1.0 + sc) + sh


def _inproj_kernel(x_ref, g_ref, sc_ref, sh_ref, w_ref, o_ref, *, nchunk):
    h = _rms_mod(x_ref[0], g_ref[...], sc_ref[0], sh_ref[0]).astype(BF16)
    for n in range(NP // nchunk):
        o_ref[0, :, n * nchunk:(n + 1) * nchunk] = _dot(h, w_ref[:, n * nchunk:(n + 1) * nchunk])


def _mod_spec(mod, tm, d, ngrid):
    per_row = mod.shape[1] != 1
    if ngrid == 2:
        imap = (lambda b, i: (b, i, 0)) if per_row else (lambda b, i: (b, 0, 0))
    else:
        imap = (lambda b, i, f: (b, i, 0)) if per_row else (lambda b, i, f: (b, 0, 0))
    return pl.BlockSpec((1, tm if per_row else 1, d), imap)


def _inproj(x, g, sc, sh, w, tm):
    bn, t, d = x.shape
    return pl.pallas_call(
        functools.partial(_inproj_kernel, nchunk=1024),
        grid=(bn, t // tm),
        in_specs=[pl.BlockSpec((1, tm, d), lambda b, i: (b, i, 0)),
                  pl.BlockSpec((1, d), lambda b, i: (0, 0)),
                  _mod_spec(sc, tm, d, 2),
                  _mod_spec(sh, tm, d, 2),
                  pl.BlockSpec((d, NP), lambda b, i: (0, 0))],
        out_specs=pl.BlockSpec((1, tm, NP), lambda b, i: (b, i, 0)),
        out_shape=jax.ShapeDtypeStruct((bn, t, NP), F32),
        compiler_params=_params(("parallel", "parallel")),
        name="inproj",
    )(x, g.reshape(1, d), sc, sh, w)


def _tail_kernel(x_ref, ya_ref, yb_ref, yc_ref, gt1_ref, sh2_ref, sc2_ref, gt2_ref, g2_ref, gf_ref,
                 wo_ref, wg_ref, wu_ref, wd_ref, o_ref, x1_s, h_s, acc_s, *, nf, final_norm):
    f = pl.program_id(2)

    @pl.when(f == 0)
    def _():
        mix = (_dot(ya_ref[0].astype(BF16), wo_ref[0:A_WIDTH])
               + _dot(yb_ref[0].astype(BF16), wo_ref[A_WIDTH:A_WIDTH + B_WIDTH])
               + _dot(yc_ref[0].astype(BF16), wo_ref[A_WIDTH + B_WIDTH:A_WIDTH + B_WIDTH + C_WIDTH]))
        x1 = x_ref[0] + gt1_ref[0] * mix
        x1_s[...] = x1
        h_s[...] = _rms_mod(x1, g2_ref[...], sc2_ref[0], sh2_ref[0]).astype(BF16)
        acc_s[...] = jnp.zeros_like(acc_s)

    hb = h_s[...]
    gate = _dot(hb, wg_ref[...])
    up = _dot(hb, wu_ref[...])
    act = (gate * _sigmoid(gate)) * up
    acc_s[...] += _dot(act.astype(BF16), wd_ref[...])

    @pl.when(f == nf - 1)
    def _():
        y = x1_s[...] + gt2_ref[0] * acc_s[...]
        if final_norm:
            ms = jnp.mean(y * y, axis=-1, keepdims=True)
            y = y * lax.rsqrt(ms + NORM_EPS) * gf_ref[...]
        o_ref[0] = y


def _tail(x, ya, yb, yc, gt1, sh2, sc2, gt2, g2, gf, wo, wgu, wd, tm, tf, final_norm):
    bn, t, d = x.shape
    dff = wd.shape[0]
    nf = dff // tf
    row = lambda w: pl.BlockSpec((1, tm, w), lambda b, i, f: (b, i, 0))
    vec = pl.BlockSpec((1, d), lambda b, i, f: (0, 0))
    return pl.pallas_call(
        functools.partial(_tail_kernel, nf=nf, final_norm=final_norm),
        grid=(bn, t // tm, nf),
        in_specs=[row(d), row(A_WIDTH), row(B_WIDTH), row(C_WIDTH),
                  _mod_spec(gt1, tm, d, 3), _mod_spec(sh2, tm, d, 3), _mod_spec(sc2, tm, d, 3),
                  _mod_spec(gt2, tm, d, 3), vec, vec,
                  pl.BlockSpec((d, d), lambda b, i, f: (0, 0)),
                  pl.BlockSpec((d, tf), lambda b, i, f: (0, f)),
                  pl.BlockSpec((d, tf), lambda b, i, f: (0, f + nf)),
                  pl.BlockSpec((tf, d), lambda b, i, f: (f, 0))],
        out_specs=pl.BlockSpec((1, tm, d), lambda b, i, f: (b, i, 0)),
        out_shape=jax.ShapeDtypeStruct((bn, t, d), F32),
        scratch_shapes=[pltpu.VMEM((tm, d), F32), pltpu.VMEM((tm, d), BF16), pltpu.VMEM((tm, d), F32)],
        compiler_params=_params(("parallel", "parallel", "arbitrary")),
        name="tail",
    )(x, ya, yb, yc, gt1, sh2, sc2, gt2, g2.reshape(1, d), gf.reshape(1, d), wo, wgu, wgu, wd)


def _unit_lower_inverses(mats, nil, mm, pre):
    n = mats[0].shape[0]
    eye = (_iota2((n, n), 0) == _iota2((n, n), 1)).astype(F32)
    p = [-a for a in mats]
    t = [eye + m for m in p]
    k = 2
    while k < nil:
        ps = [pre(x) for x in p]
        p = [mm(x, x) for x in ps]
        t = [x + mm(x, y) for x, y in zip(t, p)]
        k *= 2
    return t


def _tri_masks(c):
    rows = _iota2((c, c), 0)
    cols = _iota2((c, c), 1)
    return rows >= cols, rows > cols, rows <= cols


def _block_tri_masks(nh, c):
    n = nh * c
    rows = _iota2((n, n), 0)
    cols = _iota2((n, n), 1)
    same = (rows // c) == (cols // c)
    return jnp.logical_and(same, rows >= cols), jnp.logical_and(same, rows > cols)


def _head_blocks(x, nh):
    t, n = x.shape
    w = n // nh
    tiled = jnp.concatenate([x] * nh, axis=0)
    keep = (_iota2((nh * t, n), 0) // t) == (_iota2((nh * t, n), 1) // w)
    return jnp.where(keep, tiled, jnp.zeros_like(tiled))


def _sum_row_blocks(x, nh):
    t = x.shape[0] // nh
    out = x[0:t]
    for h in range(1, nh):
        out = out + x[h * t:(h + 1) * t]
    return out


def _scan_chunks(lg_ref, bh_ref, st_s, y_s, c, ncb, heads, dv):
    s = [st_s[h] for h in range(heads)]
    for ck in range(ncb):
        for h in range(heads):
            s_s = _split2(s[h])
            y_s[ck * c:(ck + 1) * c, h * dv:(h + 1) * dv] = (
                _dot(lg_ref[0, ck, h, 0:c].astype(BF16), s_s[0]) + bh_ref[0, ck, h, 0:c])
            s[h] = _mm(lg_ref[0, ck, h, c:], s_s) + bh_ref[0, ck, h, c:]
    for h in range(heads):
        st_s[h] = s[h]


def _scan_specs(heads, c, dk, dv, ncb):
    return [pl.BlockSpec((1, ncb, heads, c + dk, dk), lambda b, i: (b, i, 0, 0, 0)),
            pl.BlockSpec((1, ncb, heads, c + dk, dv), lambda b, i: (b, i, 0, 0, 0)),
            pl.BlockSpec((1, heads, dk, dv), lambda b, i: (b, 0, 0, 0))]


def _prev_rows(proj, t, rb, width, colblk):
    bn = proj.shape[0]
    if t == rb:
        return jnp.zeros((bn, SUB, width), F32), pl.BlockSpec((1, SUB, width), lambda b, i: (b, 0, 0))
    return proj, pl.BlockSpec((1, SUB, width),
                              lambda b, i: (b, jnp.maximum(i * (rb // SUB) - 1, 0), colblk))


def _rwkv_a_kernel(p_ref, pp_ref, sh0_ref, mu_ref, w0_ref, w2_ref, a0_ref, a2_ref, g2_ref, kk_ref, ka_ref,
                   rk_ref, lg_ref, bh_ref, eg_ref, ext_s, *, tb, c, nck):
    i = pl.program_id(1)
    nw = A_WIDTH
    rc = nck * c
    rb = nck * tb

    prev = jnp.where(i == 0, sh0_ref[0], pp_ref[0, SUB - 1:SUB, :])
    ext_s[SUB - 1:SUB] = prev
    ext_s[SUB:SUB + rb] = p_ref[0]
    if tb < c:
        ext_s[SUB + rb:SUB + rc] = jnp.zeros((rc - rb, ext_s.shape[1]), F32)
    p = ext_s[SUB:SUB + rc]
    p_prev = ext_s[SUB - 1:SUB - 1 + rc]
    m = p + (p_prev - p) * mu_ref[...]

    r = m[:, 0:nw]
    k = m[:, nw:2 * nw]
    v = m[:, 2 * nw:3 * nw]
    pw = m[:, 3 * nw:3 * nw + 64]
    pa = m[:, 3 * nw + 64:3 * nw + 128]
    pg = m[:, 3 * nw + 128:3 * nw + 256]

    w = -_softplus(-(w0_ref[...] + _mm1(jnp.tanh(pw), w2_ref[...]))) - 0.5
    logd = -jnp.exp(w)
    a = _sigmoid(a0_ref[...] + _mm1(pa, a2_ref[...]))
    g = _mm1(_sigmoid(pg), g2_ref[...])

    ones_blk = ((_iota2((nw, nw), 0) // A_DIM) == (_iota2((nw, nw), 1) // A_DIM)).astype(BF16)
    hsum = lambda t: _mm01(t, ones_blk)

    kkf = k * kk_ref[...]
    kk = kkf / jnp.maximum(jnp.sqrt(hsum(kkf * kkf)), 1e-12)
    k2 = k * (1.0 + (a - 1.0) * ka_ref[...])

    if tb < c:
        valid = _iota2((rc, nw), 0) < tb
        zero = jnp.zeros((rc, nw), F32)
        logd = jnp.where(valid, logd, zero)
        kk = jnp.where(valid, kk, zero)
        k2 = jnp.where(valid, k2, zero)
        v = jnp.where(valid, v, zero)

    bonus = hsum(r * k2 * rk_ref[...]) * v
    eg_ref[0, :, 0:nw] = bonus[0:rb]
    eg_ref[0, :, nw:2 * nw] = g[0:rb]

    lincl = _tri_masks(c)[0].astype(BF16)
    nh = A_HEADS
    hc = nh * c
    incl, strict = _block_tri_masks(nh, c)
    zb = jnp.zeros((hc, hc), F32)
    eye = (_iota2((nw, nw), 0) == _iota2((nw, nw), 1)).astype(F32)
    each = lambda f, *lists: [f(*xs) for xs in zip(*lists)]
    rs = [slice(ck * c, (ck + 1) * c) for ck in range(nck)]
    logd_c = [logd[s] for s in rs]
    cum = each(lambda x: _mm01(x, lincl, e_left=True), logd_c)
    e_in = each(jnp.exp, cum)
    e_ex = each(lambda x, y: jnp.exp(x - y), cum, logd_c)
    e_neg = each(lambda x: jnp.exp(-x), cum)
    pc = [x[c - 1:c] for x in e_in]
    kkt = each(lambda s, e: _head_blocks(kk[s] * e, nh), rs, e_ex)
    rt = each(lambda s, e: _head_blocks(r[s] * e, nh), rs, e_in)
    kah = each(lambda s, e: _head_blocks(kk[s] * a[s] * e, nh), rs, e_neg)
    kh = each(lambda s, e: _head_blocks(k2[s] * e, nh), rs, e_neg)
    mm = _mm1
    b16 = lambda x: x.astype(BF16)
    v_s = each(lambda s: b16(_head_blocks(v[s], nh)), rs)
    pr = each(lambda x1, x2, y1, y2: mm(jnp.concatenate([x1, x2], axis=0), jnp.concatenate([y1, y2], axis=0),
                                        _NT), kkt, rt, kah, kh)
    la = [jnp.where(strict, x[0:hc, 0:hc], zb) for x in pr]
    lk = [jnp.where(strict, x[0:hc, hc:2 * hc], zb) for x in pr]
    ma = [jnp.where(incl, x[hc:2 * hc, 0:hc], zb) for x in pr]
    mk = [jnp.where(incl, x[hc:2 * hc, hc:2 * hc], zb) for x in pr]
    tinv = _unit_lower_inverses(la, c, mm, b16)
    lkv = each(mm, lk, v_s)
    w1b1 = each(lambda t, x, y: b16(-mm(t, jnp.concatenate([x, y], axis=1))), tinv, kkt, lkv)
    mw = each(mm, ma, w1b1)
    mkv = each(mm, mk, v_s)
    gh = each(lambda x, p, y: mm(x * p, y, _TN), kah, pc, w1b1)
    khv = each(lambda x, p, y: mm(x * p, y, _TN), kh, pc, v_s)
    for ck in range(nck):
        lg_ref[0, ck, 0:c] = _sum_row_blocks(rt[ck] + mw[ck][:, 0:nw], nh)
        lg_ref[0, ck, c:c + A_DIM] = _sum_row_blocks(eye * pc[ck] + gh[ck][:, 0:nw], nh)
        bh_ref[0, ck, 0:c] = _sum_row_blocks(mw[ck][:, nw:2 * nw] + mkv[ck], nh)
        bh_ref[0, ck, c:c + A_DIM] = _sum_row_blocks(gh[ck][:, nw:2 * nw] + khv[ck], nh)


def _rwkv_b_kernel(lg_ref, bh_ref, s0_ref, eg_ref, lw_ref, lb_ref, y_ref, sn_ref, st_s, y_s, *, tb, c, ncb):
    nw = A_WIDTH
    nh = A_HEADS

    @pl.when(pl.program_id(1) == 0)
    def _():
        st_s[...] = _head_blocks(s0_ref[0], nh)

    s = st_s[...]
    rbk = c + A_DIM
    for ck in range(ncb):
        s_s = _split2(s)
        y_blocks = _dot(_head_blocks(lg_ref[0, ck, 0:c], nh).astype(BF16), s_s[0])
        y_s[ck * c:(ck + 1) * c] = bh_ref[0, ck, 0:c] + _sum_row_blocks(y_blocks, nh)
        s = _mm(_head_blocks(lg_ref[0, ck, c:rbk], nh), s_s) + _head_blocks(bh_ref[0, ck, c:rbk], nh)
    st_s[...] = s
    rb = ncb * tb
    y = y_s[...]
    ones_blk = ((_iota2((nw, nw), 0) // A_DIM) == (_iota2((nw, nw), 1) // A_DIM)).astype(BF16)
    hsum = lambda t: _mm01(t, ones_blk)
    mean = hsum(y) * (1.0 / A_DIM)
    yc = y - mean
    var = hsum(yc * yc) * (1.0 / A_DIM)
    yn = yc * lax.rsqrt(var + A_LNX_EPS) * lw_ref[...] + lb_ref[...]
    eg = eg_ref[0]
    y_ref[0] = (yn[0:rb] + eg[:, 0:nw]) * eg[:, nw:2 * nw]
    sn_ref[0] = _sum_row_blocks(s, nh)


def _rwkv(proj, shift0, s0, lp, c, nck, ncb):
    bn, t, _ = proj.shape
    tb = min(t, c)
    nck = nck if tb == c else 1
    ncb = ncb if tb == c else 1
    nc = t // tb
    cols = 4 * A_WIDTH
    rb = nck * tb
    vec = lambda n: pl.BlockSpec((1, n), lambda b, i: (0, 0))
    mat = lambda r, n: pl.BlockSpec((r, n), lambda b, i: (0, 0))
    row = lambda a: a.reshape(1, -1)
    pp, pp_spec = _prev_rows(proj, t, rb, cols, COL_A // cols)
    lg_shape = jax.ShapeDtypeStruct((bn, nc, c + A_DIM, A_WIDTH), F32)
    lg_spec = pl.BlockSpec((1, nck, c + A_DIM, A_WIDTH), lambda b, i: (b, i, 0, 0))
    lg, bh, eg = pl.pallas_call(
        functools.partial(_rwkv_a_kernel, tb=tb, c=c, nck=nck),
        grid=(bn, nc // nck),
        in_specs=[pl.BlockSpec((1, rb, cols), lambda b, i: (b, i, COL_A // cols)),
                  pp_spec,
                  pl.BlockSpec((1, 1, cols), lambda b, i: (b, 0, 0)),
                  vec(cols), vec(A_WIDTH), mat(64, A_WIDTH), vec(A_WIDTH), mat(64, A_WIDTH), mat(128, A_WIDTH),
                  vec(A_WIDTH), vec(A_WIDTH), vec(A_WIDTH)],
        out_specs=[lg_spec, lg_spec, pl.BlockSpec((1, rb, 2 * A_WIDTH), lambda b, i: (b, i, 0))],
        out_shape=[lg_shape, lg_shape, jax.ShapeDtypeStruct((bn, t, 2 * A_WIDTH), F32)],
        scratch_shapes=[pltpu.VMEM((SUB + nck * c, cols), F32)],
        compiler_params=_params(("parallel", "parallel")),
        name="rwkv_a",
    )(proj, pp, shift0.reshape(bn, 1, cols), row(lp["a_mu"]), row(lp["a_w0"]), lp["a_w2"], row(lp["a_a0"]),
      lp["a_a2"], lp["a_g2"], row(lp["a_k_k"]), row(lp["a_k_a"]), row(lp["a_r_k"]))
    y, st = pl.pallas_call(
        functools.partial(_rwkv_b_kernel, tb=tb, c=c, ncb=ncb),
        grid=(bn, nc // ncb),
        in_specs=[pl.BlockSpec((1, ncb, c + A_DIM, A_WIDTH), lambda b, i: (b, i, 0, 0)),
                  pl.BlockSpec((1, ncb, c + A_DIM, A_WIDTH), lambda b, i: (b, i, 0, 0)),
                  pl.BlockSpec((1, A_DIM, A_WIDTH), lambda b, i: (b, 0, 0)),
                  pl.BlockSpec((1, ncb * tb, 2 * A_WIDTH), lambda b, i: (b, i, 0)), vec(A_WIDTH), vec(A_WIDTH)],
        out_specs=[pl.BlockSpec((1, ncb * tb, A_WIDTH), lambda b, i: (b, i, 0)),
                   pl.BlockSpec((1, A_DIM, A_WIDTH), lambda b, i: (b, 0, 0))],
        out_shape=[jax.ShapeDtypeStruct((bn, t, A_WIDTH), F32),
                   jax.ShapeDtypeStruct((bn, A_DIM, A_WIDTH), F32)],
        scratch_shapes=[pltpu.VMEM((A_WIDTH, A_WIDTH), F32), pltpu.VMEM((ncb * c, A_WIDTH), F32)],
        compiler_params=_params(("parallel", "arbitrary")),
        name="rwkv_b",
    )(lg, bh, jnp.transpose(s0, (0, 3, 1, 2)).reshape(bn, A_DIM, A_WIDTH), eg,
      row(lp["a_lnx_w"]), row(lp["a_lnx_b"]))
    return y, jnp.transpose(st.reshape(bn, A_DIM, A_HEADS, A_DIM), (0, 2, 3, 1))


def _gdn_a_kernel(qkv_ref, qp_ref, bg_ref, cv0_ref, cw_ref, alog_ref, dtb_ref, lg_ref, bh_ref, ext_s, bg_s,
                  *, tb, c, nck):
    i = pl.program_id(1)
    rc = nck * c
    rb = nck * tb

    ext_s[0:SUB] = jnp.where(i == 0, cv0_ref[0], qp_ref[0])
    ext_s[SUB:SUB + rb] = qkv_ref[0]
    if tb < c:
        ext_s[SUB + rb:SUB + rc] = jnp.zeros((rc - rb, ext_s.shape[1]), F32)
    cw = cw_ref[...]
    u = ext_s[SUB - 3:SUB - 3 + rc] * cw[0:1]
    for j in range(1, B_CONV):
        u = u + ext_s[SUB - 3 + j:SUB - 3 + j + rc] * cw[j:j + 1]
    qkv = u * _sigmoid(u)

    nb = bg_s.shape[1]
    bg_s[0:rb] = bg_ref[0]
    if tb < c:
        bg_s[rb:rc] = jnp.zeros((rc - rb, nb), F32)
    bg = bg_s[...]
    beta_all = _sigmoid(bg)
    gl = -jnp.exp(alog_ref[...]) * _softplus(bg + dtb_ref[...])
    if tb < c:
        valid = _iota2((rc, nb), 0) < tb
        beta_all = jnp.where(valid, beta_all, jnp.zeros_like(beta_all))
        gl = jnp.where(valid, gl, jnp.zeros_like(gl))
    gl = gl[:, 0:LANE]

    lincl, _, upper = _tri_masks(c)
    nh = B_HEADS
    hc = nh * c
    incl, strict = _block_tri_masks(nh, c)
    zb = jnp.zeros((hc, hc), F32)
    own_head = (_iota2((hc, B_WIDTH), 0) // c) == (_iota2((hc, B_WIDTH), 1) // B_DIM)
    eye_rows = (_iota2((B_WIDTH, B_DIM), 0) % B_DIM) == _iota2((B_WIDTH, B_DIM), 1)
    each = lambda f, *lists: [f(*xs) for xs in zip(*lists)]
    rs = [slice(ck * c, (ck + 1) * c) for ck in range(nck)]
    heads_on_rows = lambda s, col0: jnp.concatenate(
        [qkv[s, col0 + h * B_DIM:col0 + (h + 1) * B_DIM] for h in range(nh)], axis=0)
    l2n = lambda x: x * lax.rsqrt(jnp.sum(x * x, axis=-1, keepdims=True) + L2_EPS)
    gc = each(lambda s: _mm01(gl[s], lincl.astype(BF16), e_left=True), rs)
    gct = each(lambda s: _mm01(gl[s], upper.astype(BF16), _TN), rs)
    q_r = each(lambda s: l2n(heads_on_rows(s, 0)) * (B_DIM ** -0.5), rs)
    k_r = each(lambda s: l2n(heads_on_rows(s, B_WIDTH)), rs)
    v_r = each(lambda s: heads_on_rows(s, 2 * B_WIDTH), rs)
    gcol = [jnp.concatenate([x[:, nh + h:nh + h + 1] for h in range(nh)], axis=0) for x in gc]
    grow = [jnp.concatenate([x[nh + h:nh + h + 1, :] for h in range(nh)], axis=1) for x in gct]
    bcol = [jnp.concatenate([beta_all[s, h:h + 1] for h in range(nh)], axis=0) for s in rs]
    glast = [jnp.concatenate([jnp.broadcast_to(x[c - 1:c, nh + h:nh + h + 1], (c, 1)) for h in range(nh)],
                             axis=0) for x in gc]
    eglast = [jnp.concatenate([jnp.broadcast_to(jnp.exp(x[c - 1:c, nh + h:nh + h + 1]), (B_DIM, 1))
                               for h in range(nh)], axis=0) for x in gc]
    decay = each(lambda x, y: jnp.where(incl, jnp.exp(jnp.where(incl, x - y, zb)), zb), gcol, grow)
    kb = each(lambda x, y: x * y, k_r, bcol)
    k_s = each(_split2, k_r)
    amat = each(lambda x, y, d: jnp.where(strict, _mm(x, y, _NT) * d, zb), kb, k_s, decay)
    qk = each(lambda x, y, d: _mm(x, y, _NT) * d, q_r, k_s, decay)
    tinv = _unit_lower_inverses(amat, c, _mm, _split2)
    eg = each(jnp.exp, gcol)
    uu = each(lambda t, x, b, y, e: _split2(_mm(t, jnp.concatenate([x * b, y * e], axis=1))),
              tinv, v_r, bcol, kb, eg)
    qu = each(_mm, qk, uu)
    kd_blk = each(lambda x, gl_, gc_: jnp.where(own_head, jnp.concatenate([x * jnp.exp(gl_ - gc_)] * nh, axis=1),
                                                jnp.zeros((hc, B_WIDTH), F32)), k_r, glast, gcol)
    ku = each(lambda x, y: _mm(x, y, _TN), kd_blk, uu)
    for ck in range(nck):
        w2 = q_r[ck] * eg[ck] - qu[ck][:, B_DIM:]
        g = jnp.where(eye_rows, eglast[ck], jnp.zeros((B_WIDTH, B_DIM), F32)) - ku[ck][:, B_DIM:]
        for h in range(nh):
            lg_ref[0, ck, h, 0:c] = w2[h * c:(h + 1) * c]
            lg_ref[0, ck, h, c:c + B_DIM] = g[h * B_DIM:(h + 1) * B_DIM]
            bh_ref[0, ck, h, 0:c] = qu[ck][h * c:(h + 1) * c, 0:B_DIM]
            bh_ref[0, ck, h, c:c + B_DIM] = ku[ck][h * B_DIM:(h + 1) * B_DIM, 0:B_DIM]


def _gdn_b_kernel(lg_ref, bh_ref, s0_ref, z_ref, nw_ref, y_ref, sn_ref, st_s, y_s, *, tb, c, ncb):
    @pl.when(pl.program_id(1) == 0)
    def _():
        st_s[...] = s0_ref[0]

    _scan_chunks(lg_ref, bh_ref, st_s, y_s, c, ncb, B_HEADS, B_DIM)
    rb = ncb * tb
    z = z_ref[0]
    for h in range(B_HEADS):
        sl = slice(h * B_DIM, (h + 1) * B_DIM)
        o = y_s[:, sl]
        on = o * lax.rsqrt(jnp.mean(o * o, axis=-1, keepdims=True) + NORM_EPS) * nw_ref[...]
        z_h = z[:, sl]
        y_ref[0, :, sl] = on[0:rb] * (z_h * _sigmoid(z_h))
    sn_ref[0] = st_s[...]


def _gdn(proj, conv0, s0, lp, c, nck, ncb):
    bn, t, _ = proj.shape
    tb = min(t, c)
    nck = nck if tb == c else 1
    ncb = ncb if tb == c else 1
    nc = t // tb
    rb = nck * tb
    nb = NP - COL_BG
    cv0 = jnp.concatenate([jnp.zeros((bn, SUB - (B_CONV - 1), B_QKV), F32), conv0], axis=1)
    lane8 = lambda a: jnp.zeros((1, nb), F32).at[0, B_HEADS:2 * B_HEADS].set(a)
    qp, qp_spec = _prev_rows(proj, t, rb, B_QKV, COL_BQKV // B_QKV)
    lg_shape = jax.ShapeDtypeStruct((bn, nc, B_HEADS, c + B_DIM, B_DIM), F32)
    lg_spec = pl.BlockSpec((1, nck, B_HEADS, c + B_DIM, B_DIM), lambda b, i: (b, i, 0, 0, 0))
    lg, bh = pl.pallas_call(
        functools.partial(_gdn_a_kernel, tb=tb, c=c, nck=nck),
        grid=(bn, nc // nck),
        in_specs=[pl.BlockSpec((1, rb, B_QKV), lambda b, i: (b, i, COL_BQKV // B_QKV)),
                  qp_spec,
                  pl.BlockSpec((1, rb, nb), lambda b, i: (b, i, COL_BG // nb)),
                  pl.BlockSpec((1, SUB, B_QKV), lambda b, i: (b, 0, 0)),
                  pl.BlockSpec((B_CONV, B_QKV), lambda b, i: (0, 0)),
                  pl.BlockSpec((1, nb), lambda b, i: (0, 0)),
                  pl.BlockSpec((1, nb), lambda b, i: (0, 0))],
        out_specs=[lg_spec, lg_spec],
        out_shape=[lg_shape, lg_shape],
        scratch_shapes=[pltpu.VMEM((SUB + nck * c, B_QKV), F32), pltpu.VMEM((nck * c, nb), F32)],
        compiler_params=_params(("parallel", "parallel")),
        name="gdn_a",
    )(proj, qp, proj, cv0, lp["b_conv_w"], lane8(lp["b_a_log"]), lane8(lp["b_dt_bias"]))
    return pl.pallas_call(
        functools.partial(_gdn_b_kernel, tb=tb, c=c, ncb=ncb),
        grid=(bn, nc // ncb),
        in_specs=_scan_specs(B_HEADS, c, B_DIM, B_DIM, ncb)
                 + [pl.BlockSpec((1, ncb * tb, B_WIDTH), lambda b, i: (b, i, COL_BZ // B_WIDTH)),
                    pl.BlockSpec((1, B_DIM), lambda b, i: (0, 0))],
        out_specs=[pl.BlockSpec((1, ncb * tb, B_WIDTH), lambda b, i: (b, i, 0)),
                   pl.BlockSpec((1, B_HEADS, B_DIM, B_DIM), lambda b, i: (b, 0, 0, 0))],
        out_shape=[jax.ShapeDtypeStruct((bn, t, B_WIDTH), F32),
                   jax.ShapeDtypeStruct((bn, B_HEADS, B_DIM, B_DIM), F32)],
        scratch_shapes=[pltpu.VMEM((B_HEADS, B_DIM, B_DIM), F32), pltpu.VMEM((ncb * c, B_WIDTH), F32)],
        compiler_params=_params(("parallel", "arbitrary")),
        name="gdn_b",
    )(lg, bh, s0, proj, lp["b_norm_w"].reshape(1, B_DIM))


def _sbp_kernel(bias_ref, q_ref, k_ref, v_ref, o_ref, carry_s, acc_s, *, tq, tk):
    qi = pl.program_id(1)
    j = pl.program_id(2)
    nsub = tq // tk

    @pl.when(j == 0)
    def _():
        carry_s[...] = jnp.zeros_like(carry_s)
        acc_s[...] = jnp.zeros_like(acc_s)

    def sweep(diag):
        qb = (q_ref[0] * (C_DIM ** -0.5 * LOG2E)).astype(BF16)
        kb = k_ref[0].astype(BF16)
        vb = v_ref[0].astype(BF16)
        tri = (_iota2((tk, 2 * tk), 1) % tk >= _iota2((tk, 2 * tk), 0)).astype(BF16)
        hsl = lambda h: slice(h * C_DIM, (h + 1) * C_DIM)
        tiles = [(s, h) for s in reversed(range(nsub)) for h in range(C_HEADS)]
        q0 = {s: s * tk if diag else 0 for s in range(nsub)}
        valid = {s: _iota2((tk, tq - q0[s]), 0) < _iota2((tk, tq - q0[s]), 1) for s in range(nsub)}
        zt = {(s, h): _dot_nt(kb[s * tk:(s + 1) * tk, hsl(h)], qb[q0[s]:, hsl(h)]) + bias_ref[h] * LOG2E
              for s, h in tiles}
        rc = {}
        for s, h in tiles:
            z = zt[s, h]
            sp = jnp.maximum(z, 0.0) + jnp.log2(1.0 + jnp.exp2(-jnp.abs(z)))
            if diag:
                sp = jnp.where(valid[s], sp, jnp.zeros_like(sp))
            hi, lo = _split2(sp)
            rc[s, h] = _dot(tri, jnp.concatenate([hi, lo], axis=0))
        carry = [carry_s[h:h + 1, :] for h in range(C_HEADS)]
        acc = [acc_s[h] for h in range(C_HEADS)]
        for s, h in tiles:
            at = jnp.exp2(zt[s, h] - rc[s, h] - carry[h][:, q0[s]:])
            if diag:
                at = jnp.where(valid[s], at, jnp.zeros_like(at))
            o_t = _dot_tn(vb[s * tk:(s + 1) * tk, hsl(h)], at.astype(BF16))
            c_t = rc[s, h][0:1]
            if q0[s]:
                o_t = jnp.concatenate([jnp.zeros((C_DIM, q0[s]), F32), o_t], axis=1)
                c_t = jnp.concatenate([jnp.zeros((1, q0[s]), F32), c_t], axis=1)
            acc[h] = acc[h] + o_t
            carry[h] = carry[h] + c_t
        for h in range(C_HEADS):
            carry_s[h:h + 1, :] = carry[h]
            acc_s[h] = acc[h]

    @pl.when(j == 0)
    def _():
        sweep(True)

    @pl.when(jnp.logical_and(j > 0, j <= qi))
    def _():
        sweep(False)

    @pl.when(j == qi)
    def _():
        o_ref[0] = acc_s[...].reshape(C_WIDTH, tq).T


def _sb_prompt(proj, bias, tq, tk):
    bn, t, _ = proj.shape
    nq = t // tq
    qcol = COL_C // C_WIDTH
    return pl.pallas_call(
        functools.partial(_sbp_kernel, tq=tq, tk=tk),
        grid_spec=pltpu.PrefetchScalarGridSpec(
            num_scalar_prefetch=0,
            grid=(bn, nq, nq),
            in_specs=[pl.BlockSpec(memory_space=pltpu.SMEM),
                      pl.BlockSpec((1, tq, C_WIDTH), lambda b, i, j: (b, i, qcol)),
                      pl.BlockSpec((1, tq, C_WIDTH), lambda b, i, j: (b, jnp.maximum(i - j, 0), qcol + 1)),
                      pl.BlockSpec((1, tq, C_WIDTH), lambda b, i, j: (b, jnp.maximum(i - j, 0), qcol + 2))],
            out_specs=pl.BlockSpec((1, tq, C_WIDTH), lambda b, i, j: (b, i, 0)),
            scratch_shapes=[pltpu.VMEM((C_HEADS, tq), F32), pltpu.VMEM((C_HEADS, C_DIM, tq), F32)]),
        out_shape=jax.ShapeDtypeStruct((bn, t, C_WIDTH), F32),
        compiler_params=_params(("parallel", "parallel", "arbitrary")),
        name="sbp",
    )(bias, proj, proj, proj)


def _sbs_kernel(pt_ref, bias_ref, q_ref, k_ref, v_ref, *rest, tn, npage, ps):
    kp_refs = rest[0:npage]
    vp_refs = rest[npage:2 * npage]
    o_ref, q_s, carry_s, acc_s, kown_s, vown_s = rest[2 * npage:]
    j = pl.program_id(1)
    nj = pl.num_programs(1)
    rows = C_HEADS * SUB
    tri = (_iota2((ps, ps), 0) >= _iota2((ps, ps), 1)).astype(BF16)
    bias_col = jnp.concatenate([jnp.full((SUB, 1), bias_ref[h], F32) for h in range(C_HEADS)], axis=0)
    hsl = lambda h: slice(h * C_DIM, (h + 1) * C_DIM)
    rsl = lambda h: slice(h * SUB, (h + 1) * SUB)

    @pl.when(j == 0)
    def _():
        q_s[...] = jnp.zeros_like(q_s)
        q_s[0:tn] = q_ref[0] * (C_DIM ** -0.5)
        kown_s[...] = jnp.zeros_like(kown_s)
        vown_s[...] = jnp.zeros_like(vown_s)
        kown_s[0:tn] = k_ref[0]
        vown_s[0:tn] = v_ref[0]
        qb = q_s[...].astype(BF16)
        kb = kown_s[...].astype(BF16)
        vb = vown_s[...].astype(BF16)
        z = jnp.concatenate([_dot_nt(qb[:, hsl(h)], kb[:, hsl(h)]) for h in range(C_HEADS)], axis=0) + bias_col
        valid = _iota2((rows, ps), 1) < (_iota2((rows, ps), 0) % SUB)
        zero = jnp.zeros((rows, ps), F32)
        rc = _mm01(jnp.where(valid, _softplus(z), zero), tri)
        a = jnp.where(valid, jnp.exp(z - rc), zero)
        for h in range(C_HEADS):
            acc_s[h] = _dot(a[rsl(h)].astype(BF16), vb[:, hsl(h)])
        carry_s[...] = rc[:, 0:1]

    qb = q_s[...].astype(BF16)
    kt = [kp_refs[i][0, 0].astype(BF16) for i in range(npage)]
    vt = [vp_refs[i][0, 0].astype(BF16) for i in range(npage)]
    z = jnp.concatenate(
        [_dot(qb[:, hsl(h)], jnp.concatenate([kt[i][h] for i in range(npage)], axis=1)) for h in range(C_HEADS)],
        axis=0) + bias_col
    sp = _softplus(z)
    sp_rows = jnp.concatenate([sp[:, i * ps:(i + 1) * ps] for i in range(npage)], axis=0)
    hi, lo = _split2(sp_rows)
    rc_rows = _dot(jnp.concatenate([hi, lo], axis=1), jnp.concatenate([tri, tri], axis=0))
    carry = carry_s[...]
    a_segs = []
    for i in range(npage):
        seg = slice(i * ps, (i + 1) * ps)
        rc = rc_rows[i * rows:(i + 1) * rows]
        a_segs.append(jnp.exp(z[:, seg] - rc - carry))
        carry = carry + rc[:, 0:1]
    carry_s[...] = carry
    a = jnp.concatenate(a_segs, axis=1)
    for h in range(C_HEADS):
        v_h = jnp.concatenate([vt[i][h] for i in range(npage)], axis=1)
        acc_s[h] += _dot_nt(a[rsl(h)].astype(BF16), v_h)

    @pl.when(j == nj - 1)
    def _():
        for h in range(C_HEADS):
            o_ref[0, :, hsl(h)] = acc_s[h][0:tn]


def _sb_sample(proj, cache_kt, cache_vt, page_table, bias, layer, npage):
    bn, tn, _ = proj.shape
    n_pages = page_table.shape[1]
    ps = cache_kt.shape[-1]
    qcol = COL_C // C_WIDTH
    page_spec = lambda i: pl.BlockSpec(
        (1, 1, C_HEADS, C_DIM, ps),
        lambda b, j, pt: (layer, pt[b, n_pages - 1 - (j * npage + i)], 0, 0, 0))
    own = lambda off: pl.BlockSpec((1, tn, C_WIDTH), lambda b, j, pt: (b, 0, qcol + off))
    return pl.pallas_call(
        functools.partial(_sbs_kernel, tn=tn, npage=npage, ps=ps),
        grid_spec=pltpu.PrefetchScalarGridSpec(
            num_scalar_prefetch=1,
            grid=(bn, n_pages // npage),
            in_specs=[pl.BlockSpec(memory_space=pltpu.SMEM), own(0), own(1), own(2)]
                     + [page_spec(i) for i in range(npage)] + [page_spec(i) for i in range(npage)],
            out_specs=pl.BlockSpec((1, tn, C_WIDTH), lambda b, j, pt: (b, 0, 0)),
            scratch_shapes=[pltpu.VMEM((SUB, C_WIDTH), F32),
                            pltpu.VMEM((C_HEADS * SUB, 1), F32),
                            pltpu.VMEM((C_HEADS, SUB, C_DIM), F32),
                            pltpu.VMEM((ps, C_WIDTH), F32),
                            pltpu.VMEM((ps, C_WIDTH), F32)]),
        out_shape=jax.ShapeDtypeStruct((bn, tn, C_WIDTH), F32),
        compiler_params=_params(("parallel", "arbitrary")),
        name="sbs",
    )(page_table, bias, proj, proj, proj, *([cache_kt] * npage), *([cache_vt] * npage))


def _reorder_w_in(w):
    a_cols = 4 * A_WIDTH
    b0 = a_cols
    bz0 = b0 + B_QKV
    bg0 = bz0 + B_WIDTH
    c0 = bg0 + 2 * B_HEADS
    c1 = c0 + 3 * C_WIDTH
    pad = jnp.zeros((w.shape[0], NP - (COL_BG + 2 * B_HEADS)), w.dtype)
    return jnp.concatenate([w[:, 0:a_cols], w[:, bz0:bg0], w[:, b0:bz0], w[:, c0:c1], w[:, bg0:c0], pad],
                           axis=1).astype(BF16)


def kernel(x_prompt, x_sample, cache_k, cache_v, state_shift, state_wkv, state_conv, state_gdn, page_table,
           c_prompt, c_sample, w_ada, b_ada, norm1, norm2, w_in, w_out, a_mu, a_w0, a_w2, a_a0, a_a2, a_g2,
           a_k_k, a_k_a, a_r_k, a_lnx_w, a_lnx_b, b_conv_w, b_a_log, b_dt_bias, b_norm_w, c_bias, w_gu, w_down,
           norm_f):
    depth = w_in.shape[0]
    nb, seq, d = x_prompt.shape
    nd, tn, _ = x_sample.shape
    n_ada = w_ada.shape[2] // d

    r_all = -(-(nb + nd) // SUB) * SUB
    c_all = jnp.concatenate([c_prompt, c_sample, jnp.zeros((r_all - nb - nd, d), F32)], axis=0)
    mod = _ada(c_all, w_ada, b_ada)

    cache_kt = jnp.transpose(cache_k, (0, 1, 3, 4, 2))
    cache_vt = jnp.transpose(cache_v, (0, 1, 3, 4, 2))

    xp = x_prompt
    xs = x_sample.reshape(1, nd * tn, d)
    outs_p = [[] for _ in range(6)]
    outs_s = [[] for _ in range(6)]
    for l in range(depth):
        lp = dict(a_mu=a_mu[l], a_w0=a_w0[l], a_w2=a_w2[l], a_a0=a_a0[l], a_a2=a_a2[l], a_g2=a_g2[l],
                  a_k_k=a_k_k[l], a_k_a=a_k_a[l], a_r_k=a_r_k[l], a_lnx_w=a_lnx_w[l], a_lnx_b=a_lnx_b[l],
                  b_conv_w=b_conv_w[l], b_a_log=b_a_log[l], b_dt_bias=b_dt_bias[l], b_norm_w=b_norm_w[l])
        w_in_l = _reorder_w_in(w_in[l])
        w_out_l = w_out[l].astype(BF16)
        w_gu_l = w_gu[l].astype(BF16)
        w_down_l = w_down[l].astype(BF16)
        last = l == depth - 1

        mods_p = [m.reshape(nb, 1, d) for m in jnp.split(mod[l, 0:nb], n_ada, axis=-1)]
        mods_s = [jnp.repeat(m, tn, axis=0).reshape(1, nd * tn, d)
                  for m in jnp.split(mod[l, nb:nb + nd], n_ada, axis=-1)]

        sh1, sc1, gt1, sh2, sc2, gt2 = mods_p
        proj = _inproj(xp, norm1[l], sc1, sh1, w_in_l, tm=256)
        ya, wkv_p = _rwkv(proj, jnp.zeros((nb, 4 * A_WIDTH), F32),
                          jnp.zeros((nb, A_HEADS, A_DIM, A_DIM), F32), lp, c=64, nck=4, ncb=8)
        yb, gdn_p = _gdn(proj, jnp.zeros((nb, B_CONV - 1, B_QKV), F32),
                         jnp.zeros((nb, B_HEADS, B_DIM, B_DIM), F32), lp, c=64, nck=4, ncb=8)
        yc = _sb_prompt(proj, c_bias[l], tq=512, tk=128)
        xp = _tail(xp, ya, yb, yc, gt1, sh2, sc2, gt2, norm2[l], norm_f, w_out_l, w_gu_l, w_down_l,
                   tm=512, tf=1408, final_norm=last)
        k_new = proj[:, :, COL_C + C_WIDTH:COL_C + 2 * C_WIDTH].reshape(nb, seq, C_HEADS, C_DIM)
        v_new = proj[:, :, COL_C + 2 * C_WIDTH:COL_C + 3 * C_WIDTH].reshape(nb, seq, C_HEADS, C_DIM)
        shift_new = proj[:, seq - 1, COL_A:COL_A + 4 * A_WIDTH]
        conv_new = proj[:, seq - (B_CONV - 1):, COL_BQKV:COL_BQKV + B_QKV]
        for lst, val in zip(outs_p, (k_new, v_new, shift_new, wkv_p, conv_new, gdn_p)):
            lst.append(val)

        sh1, sc1, gt1, sh2, sc2, gt2 = mods_s
        proj = _inproj(xs, norm1[l], sc1, sh1, w_in_l, tm=nd * tn).reshape(nd, tn, NP)
        ya, wkv_s = _rwkv(proj, state_shift[l], state_wkv[l], lp, c=SUB, nck=1, ncb=1)
        yb, gdn_s = _gdn(proj, state_conv[l], state_gdn[l], lp, c=SUB, nck=1, ncb=1)
        yc = _sb_sample(proj, cache_kt, cache_vt, page_table, c_bias[l], l, npage=64)
        flat = lambda a: a.reshape(1, nd * tn, a.shape[-1])
        xs = _tail(xs, flat(ya), flat(yb), flat(yc), gt1, sh2, sc2, gt2, norm2[l], norm_f, w_out_l, w_gu_l,
                   w_down_l, tm=nd * tn, tf=256, final_norm=last)
        k_new = proj[:, :, COL_C + C_WIDTH:COL_C + 2 * C_WIDTH].reshape(nd, tn, C_HEADS, C_DIM)
        v_new = proj[:, :, COL_C + 2 * C_WIDTH:COL_C + 3 * C_WIDTH].reshape(nd, tn, C_HEADS, C_DIM)
        shift_new = proj[:, tn - 1, COL_A:COL_A + 4 * A_WIDTH]
        conv_new = jnp.concatenate([state_conv[l], proj[:, :, COL_BQKV:COL_BQKV + B_QKV]],
                                   axis=1)[:, -(B_CONV - 1):]
        for lst, val in zip(outs_s, (k_new, v_new, shift_new, wkv_s, conv_new, gdn_s)):
            lst.append(val)

    y_prompt = xp
    y_sample = xs.reshape(nd, tn, d)
    p_out = tuple(jnp.stack(t) for t in outs_p)
    s_out = tuple(jnp.stack(t) for t in outs_s)
    return (y_prompt, y_sample) + p_out + s_out
```
